```python
import math
import jax, jax.numpy as jnp
from jax import lax
import numpy as np

D_MODEL = 1024
BATCH = 16
SEQ = 2048
DEPTH = 2

GRID_W = 64
CTX_LEN = 256
N_SUB = 3
D_FF = 2816
EPS = 1e-6
ROPE_THETA = 10000.0
ROPE_DIM = 64
Q_BLOCK = 128
DA_HEADS = 4
DA_DH = 64
DA_DV = 2 * DA_DH
RET_HEADS = 4
RET_DK = 64
RET_DV = 128
RET_CHUNK = 128
GQA_HEADS = 8
GQA_KV = 2
GQA_DH = 64
N_BRANCH = 3
BRANCH_W = 512

PROJ_NAMES = ('a_q', 'a_k', 'a_v', 'b_q', 'b_k', 'b_v', 'b_g', 'c_q', 'c_k', 'c_v', 'gate')
PROJ_SIZES = (DA_HEADS * 2 * DA_DH, DA_HEADS * 2 * DA_DH, DA_HEADS * DA_DV,
              RET_HEADS * RET_DK, RET_HEADS * RET_DK, RET_HEADS * RET_DV, RET_HEADS * RET_DV,
              GQA_HEADS * GQA_DH, GQA_KV * GQA_DH, GQA_KV * GQA_DH, N_BRANCH * D_MODEL)
IN_COLS = sum(PROJ_SIZES)
KV_NAMES = ('a_k', 'a_v', 'b_k', 'b_v', 'c_k', 'c_v')

kernel_name = 'hybrid_gated_diffattn_retention_gqa_block'


def rms_norm(x, g):
    xf = x.astype(jnp.float32)
    y = xf * lax.rsqrt(jnp.mean(xf * xf, axis=-1, keepdims=True) + EPS)
    return (y * g.astype(jnp.float32)).astype(x.dtype)


def axial_rope(n_tok, dim):
    rows = n_tok // GRID_W
    row = jnp.repeat(jnp.arange(rows, dtype=jnp.float32), GRID_W)
    col = jnp.tile(jnp.arange(GRID_W, dtype=jnp.float32), rows)
    n_freq = dim // 4
    inv = ROPE_THETA ** (-jnp.arange(n_freq, dtype=jnp.float32) / n_freq)
    ang = jnp.concatenate([row[:, None] * inv, col[:, None] * inv], axis=-1)
    return jnp.cos(ang), jnp.sin(ang)


def apply_rope(x, cos, sin):
    cos, sin = cos.astype(x.dtype), sin.astype(x.dtype)
    x1, x2 = jnp.split(x, 2, axis=-1)
    return jnp.concatenate([x1 * cos - x2 * sin, x1 * sin + x2 * cos], axis=-1)


def proj_spans():
    spans, lo = {}, 0
    for name, size in zip(PROJ_NAMES, PROJ_SIZES):
        spans[name] = (lo, lo + size)
        lo += size
    return spans


def project(y, w, names):
    spans = proj_spans()
    sel = [spans[nm] for nm in names]
    w_sel = w if names == PROJ_NAMES else jnp.concatenate([w[:, lo:hi] for lo, hi in sel], axis=1)
    p = y @ w_sel
    out, o = {}, 0
    for nm, (lo, hi) in zip(names, sel):
        out[nm] = p[..., o:o + hi - lo]
        o += hi - lo
    return out


def heads(t, n_heads, d):
    b, n, _ = t.shape
    return t.reshape(b, n, n_heads, d).transpose(0, 2, 1, 3)


def merge_heads(t):
    b, h, n, d = t.shape
    return t.transpose(0, 2, 1, 3).reshape(b, n, h * d)


def diff_heads(t):
    b, n, _ = t.shape
    return t.reshape(b, n, DA_HEADS, 2, DA_DH).transpose(0, 2, 3, 1, 4)


def gqa_q_heads(t):
    b, n, _ = t.shape
    return t.reshape(b, n, GQA_KV, GQA_HEADS // GQA_KV, GQA_DH).transpose(0, 2, 3, 1, 4)


def sweep_query_blocks(block_fn, q):
    *lead, n, d = q.shape
    nb = n // Q_BLOCK
    qb = jnp.moveaxis(q.reshape(*lead, nb, Q_BLOCK, d), -3, 0)
    ob = jnp.moveaxis(lax.map(block_fn, qb), 0, -3)
    return ob.reshape(*ob.shape[:-3], n, ob.shape[-1])


def diff_attention(q, k, v, lam):
    scale = DA_DH ** -0.5
    def block(qb):
        s = jnp.einsum('bhcqd,bhctd->bhcqt', qb, k).astype(jnp.float32) * scale
        p = jax.nn.softmax(s, axis=-1)
        a = p[:, :, 0] - lam * p[:, :, 1]
        return jnp.einsum('bhqt,bhtv->bhqv', a.astype(v.dtype), v)
    return sweep_query_blocks(block, q)


def gqa_attention(q, k, v):
    scale = GQA_DH ** -0.5
    def block(qb):
        s = jnp.einsum('bgrqd,bgtd->bgrqt', qb, k).astype(jnp.float32) * scale
        p = jax.nn.softmax(s, axis=-1)
        return jnp.einsum('bgrqt,bgtd->bgrqd', p.astype(v.dtype), v)
    return sweep_query_blocks(block, q)


def retention_chunks(q, k, v, log_gamma, state0):
    b, h, n, _ = q.shape
    dv = v.shape[-1]
    nc = n // RET_CHUNK
    pos = jnp.arange(RET_CHUNK, dtype=jnp.float32)
    rel = pos[:, None] - pos[None, :]
    intra = jnp.where(rel >= 0, jnp.exp(log_gamma[:, None, None] * jnp.maximum(rel, 0.0)), 0.0)
    q_dec = jnp.exp(log_gamma[:, None] * (pos + 1.0))[..., None]
    k_dec = jnp.exp(log_gamma[:, None] * (RET_CHUNK - 1.0 - pos))[..., None]
    s_dec = jnp.exp(log_gamma * RET_CHUNK)[:, None, None]

    def to_chunks(t):
        return jnp.moveaxis(t.astype(jnp.float32).reshape(b, h, nc, RET_CHUNK, t.shape[-1]), 2, 0)

    def step(state, qkv):
        qc, kc, vc = qkv
        att = jnp.einsum('bhid,bhjd->bhij', qc, kc) * intra
        o = jnp.einsum('bhij,bhjv->bhiv', att, vc) + jnp.einsum('bhid,bhdv->bhiv', qc * q_dec, state)
        state = state * s_dec + jnp.einsum('bhjd,bhjv->bhdv', kc * k_dec, vc)
        return state, o

    state, o = lax.scan(step, state0, (to_chunks(q), to_chunks(k), to_chunks(v)))
    return jnp.moveaxis(o, 0, 2).reshape(b, h, n, dv), state


def retention_state(k, v, log_gamma):
    n = k.shape[2]
    w = jnp.exp(log_gamma[:, None] * (n - 1.0 - jnp.arange(n, dtype=jnp.float32)))
    return jnp.einsum('bhjd,bhjv->bhdv', k.astype(jnp.float32) * w[..., None], v.astype(jnp.float32))


def sub_mod(mod, s):
    return mod[:, :, s, 0], mod[:, :, s, 1], mod[:, :, s, 2]


def ffn_sublayer(h, mod3, g_pre, g_post, w_i, w_o):
    shift, scale, gate = mod3
    y = rms_norm(h, g_pre) * (1 + scale) + shift
    a, u = jnp.split(y @ w_i, 2, axis=-1)
    y = (jax.nn.silu(a) * u) @ w_o
    return h + 0.5 * gate * rms_norm(y, g_post)


def merge_branches(outs, gate_pre, b_gate, w_branch, w_out):
    g = jax.nn.sigmoid((gate_pre + b_gate).reshape(*gate_pre.shape[:-1], N_BRANCH, D_MODEL))
    merged = sum(g[..., i, :] * (o @ w_branch[i]) for i, o in enumerate(outs))
    return merged @ w_out


def mixer_sublayer(h, hc, mod3, mod3_c, g_pre, g_post, w_in, b_gate, diff_lambda, diff_norm_g,
                   ret_logit, ret_norm_g, qk_norm_g, w_branch, w_out, lam_init, need_ctx):
    b, n, _ = h.shape
    shift, scale, gate = mod3
    cshift, cscale, cgate = mod3_c
    pl = project(rms_norm(h, g_pre) * (1 + scale) + shift, w_in, PROJ_NAMES)
    pc = project(rms_norm(hc, g_pre) * (1 + cscale) + cshift, w_in, PROJ_NAMES if need_ctx else KV_NAMES)
    cos, sin = axial_rope(n, ROPE_DIM)
    rope = lambda t: apply_rope(t, cos, sin)
    cat = lambda t_lat, t_ctx: jnp.concatenate([t_lat, t_ctx], axis=-2)
    flip = lambda t: t[:, :, ::-1]

    lq1, lk1, lq2, lk2 = diff_lambda.astype(jnp.float32)
    lam = jnp.exp(jnp.sum(lq1 * lk1)) - jnp.exp(jnp.sum(lq2 * lk2)) + lam_init
    ka_c, va_c = diff_heads(pc['a_k']), heads(pc['a_v'], DA_HEADS, DA_DV)
    diff_out = lambda o: merge_heads(rms_norm(o, diff_norm_g) * (1.0 - lam_init))
    oa = diff_out(diff_attention(rope(diff_heads(pl['a_q'])),
                                 cat(rope(diff_heads(pl['a_k'])), ka_c),
                                 cat(heads(pl['a_v'], DA_HEADS, DA_DV), va_c), lam))

    lg = jax.nn.log_sigmoid(ret_logit.astype(jnp.float32))
    kb_c, vb_c = heads(pc['b_k'], RET_HEADS, RET_DK), heads(pc['b_v'], RET_HEADS, RET_DV)
    if need_ctx:
        zero = jnp.zeros((b, RET_HEADS, RET_DK, RET_DV), jnp.float32)
        qb_c = heads(pc['b_q'], RET_HEADS, RET_DK) * RET_DK ** -0.5
        o_cf, s_f = retention_chunks(qb_c, kb_c, vb_c, lg[0], zero)
        o_cb, s_b = retention_chunks(flip(qb_c), flip(kb_c), flip(vb_c), lg[1], zero)
    else:
        s_f = retention_state(kb_c, vb_c, lg[0])
        s_b = retention_state(flip(kb_c), flip(vb_c), lg[1])
    qb = rope(heads(pl['b_q'], RET_HEADS, RET_DK)) * RET_DK ** -0.5
    kb = rope(heads(pl['b_k'], RET_HEADS, RET_DK))
    vb = heads(pl['b_v'], RET_HEADS, RET_DV)
    o_f, _ = retention_chunks(qb, kb, vb, lg[0], s_f)
    o_b, _ = retention_chunks(flip(qb), flip(kb), flip(vb), lg[1], s_b)
    ret_out = lambda o, g: merge_heads(rms_norm(o.astype(h.dtype), ret_norm_g)) * jax.nn.silu(g)
    ob = ret_out(o_f + flip(o_b), pl['b_g'])

    kc_c = rms_norm(heads(pc['c_k'], GQA_KV, GQA_DH), qk_norm_g[1])
    vc_c = heads(pc['c_v'], GQA_KV, GQA_DH)
    def gqa_out(o):
        bb, gg, rr, nn, dd = o.shape
        return o.transpose(0, 3, 1, 2, 4).reshape(bb, nn, gg * rr * dd)
    oc = gqa_out(gqa_attention(rope(rms_norm(gqa_q_heads(pl['c_q']), qk_norm_g[0])),
                               cat(rope(rms_norm(heads(pl['c_k'], GQA_KV, GQA_DH), qk_norm_g[1])), kc_c),
                               cat(heads(pl['c_v'], GQA_KV, GQA_DH), vc_c)))

    out = merge_branches((oa, ob, oc), pl['gate'], b_gate, w_branch, w_out)
    h = h + gate * rms_norm(out, g_post)
    if not need_ctx:
        return h, None

    oa_c = diff_out(diff_attention(diff_heads(pc['a_q']), ka_c, va_c, lam))
    ob_c = ret_out(o_cf + flip(o_cb), pc['b_g'])
    oc_c = gqa_out(gqa_attention(rms_norm(gqa_q_heads(pc['c_q']), qk_norm_g[0]), kc_c, vc_c))
    out_c = merge_branches((oa_c, ob_c, oc_c), pc['gate'], b_gate, w_branch, w_out)
    hc = hc + cgate * rms_norm(out_c, g_post)
    return h, hc


def setup_inputs(seed: int = 0) -> dict:
    key = jax.random.key(seed)
    ks = jax.random.split(key, 18)
    f32 = jnp.float32
    nrm = lambda k, shape, s: jax.random.normal(k, shape, f32) * s
    gamma0 = 1.0 - 2.0 ** (-5.0 - jnp.arange(RET_HEADS, dtype=f32))
    return {
        'x': nrm(ks[0], (BATCH, SEQ, D_MODEL), 1.0),
        'c': nrm(ks[1], (BATCH, D_MODEL), 1.0),
        'ctx': nrm(ks[2], (BATCH, CTX_LEN, D_MODEL), 1.0),
        'c_ctx': nrm(ks[3], (D_MODEL,), 1.0),
        'w_mod': nrm(ks[4], (DEPTH, D_MODEL, 3 * N_SUB * D_MODEL), 0.5 * D_MODEL ** -0.5),
        'b_mod': nrm(ks[5], (DEPTH, 3 * N_SUB * D_MODEL), 0.02),
        'norm_g': 1.0 + nrm(ks[6], (DEPTH, 2 * N_SUB, D_MODEL), 0.05),
        'w_ffn_in': nrm(ks[7], (DEPTH, 2, D_MODEL, 2 * D_FF), D_MODEL ** -0.5),
        'w_ffn_out': nrm(ks[8], (DEPTH, 2, D_FF, D_MODEL), D_FF ** -0.5),
        'w_in': nrm(ks[9], (DEPTH, D_MODEL, IN_COLS), D_MODEL ** -0.5),
        'b_gate': nrm(ks[10], (DEPTH, N_BRANCH * D_MODEL), 0.02),
        'diff_lambda': nrm(ks[11], (DEPTH, 4, DA_DH), 0.1),
        'diff_norm_g': 1.0 + nrm(ks[12], (DEPTH, DA_DV), 0.05),
        'ret_decay_logit': jnp.log(gamma0 / (1.0 - gamma0))[None, None, :] + nrm(ks[13], (DEPTH, 2, RET_HEADS), 0.1),
        'ret_norm_g': 1.0 + nrm(ks[14], (DEPTH, RET_DV), 0.05),
        'qk_norm_g': 1.0 + nrm(ks[15], (DEPTH, 2, GQA_DH), 0.05),
        'w_branch': nrm(ks[16], (DEPTH, N_BRANCH, BRANCH_W, D_MODEL), BRANCH_W ** -0.5),
        'w_out': nrm(ks[17], (DEPTH, D_MODEL, D_MODEL), D_MODEL ** -0.5),
    }


def reference(x, c, ctx, c_ctx, w_mod, b_mod, norm_g, w_ffn_in, w_ffn_out, w_in, b_gate,
              diff_lambda, diff_norm_g, ret_decay_logit, ret_norm_g, qk_norm_g, w_branch, w_out):
    h, hc = x, ctx
    for l in range(DEPTH):
        last = l == DEPTH - 1
        mod = (jax.nn.silu(c) @ w_mod[l] + b_mod[l]).reshape(c.shape[0], 1, N_SUB, 3, D_MODEL)
        mod_c = (jax.nn.silu(c_ctx) @ w_mod[l] + b_mod[l]).reshape(1, 1, N_SUB, 3, D_MODEL)
        lam_init = 0.8 - 0.6 * math.exp(-0.3 * l)
        h = ffn_sublayer(h, sub_mod(mod, 0), norm_g[l, 0], norm_g[l, 1], w_ffn_in[l, 0], w_ffn_out[l, 0])
        hc = ffn_sublayer(hc, sub_mod(mod_c, 0), norm_g[l, 0], norm_g[l, 1], w_ffn_in[l, 0], w_ffn_out[l, 0])
        h, hc = mixer_sublayer(h, hc, sub_mod(mod, 1), sub_mod(mod_c, 1), norm_g[l, 2], norm_g[l, 3],
                               w_in[l], b_gate[l], diff_lambda[l], diff_norm_g[l], ret_decay_logit[l],
                               ret_norm_g[l], qk_norm_g[l], w_branch[l], w_out[l], lam_init, not last)
        h = ffn_sublayer(h, sub_mod(mod, 2), norm_g[l, 4], norm_g[l, 5], w_ffn_in[l, 1], w_ffn_out[l, 1])
        if not last:
            hc = ffn_sublayer(hc, sub_mod(mod_c, 2), norm_g[l, 4], norm_g[l, 5], w_ffn_in[l, 1], w_ffn_out[l, 1])
    return h
```

```python
import functools
import math

import numpy as np
import jax
import jax.numpy as jnp
from jax import lax
from jax.experimental import pallas as pl
from jax.experimental.pallas import tpu as pltpu

F32 = jnp.float32
BF16 = jnp.bfloat16

D_MODEL = 1024
N_SUB = 3
D_FF = 2816
EPS = 1e-6
ROPE_THETA = 10000.0
ROPE_DIM = 64
GRID_W = 64
HEAD_DIM = 64
LANES = 128
DA_HEADS = 4
RET_HEADS = 4
RET_DV = 128
GQA_HEADS = 8
GQA_KV = 2
N_BRANCH = 3
BRANCH_W = 512
QK_SCALE = HEAD_DIM ** -0.5
MXU_TILE = 256
RET_CHUNK = MXU_TILE
VMEM_LIMIT = 56 * 1024 * 1024

_SIZES = (512, 512, 512, 256, 256, 512, 512, 512, 128, 128, N_BRANCH * D_MODEL)
_NAMES = ('a_q', 'a_k', 'a_v', 'b_q', 'b_k', 'b_v', 'b_g', 'c_q', 'c_k', 'c_v', 'gate')
_SPAN = {}
_lo = 0
for _n, _s in zip(_NAMES, _SIZES):
    _SPAN[_n] = (_lo, _lo + _s)
    _lo += _s
IN_COLS = _lo


def _cparams(n_axes):
    return pltpu.CompilerParams(dimension_semantics=("arbitrary",) * n_axes,
                                vmem_limit_bytes=VMEM_LIMIT)


def _const_spec(shape):
    nd = len(shape)
    return pl.BlockSpec(shape, lambda *_: (0,) * nd, pipeline_mode=pl.Buffered(1))


def _rms(x, g):
    return x * lax.rsqrt(jnp.mean(x * x, axis=-1, keepdims=True) + EPS) * g


def _sigmoid(x):
    return 1.0 / (1.0 + jnp.exp(-x))


def _dot(a, b):
    return jnp.dot(a, b, preferred_element_type=F32)


def _dot_nt(a, b):
    return lax.dot_general(a, b, (((1,), (1,)), ((), ())), preferred_element_type=F32)


def _dot_tn(a, b):
    return lax.dot_general(a, b, (((0,), (0,)), ((), ())), preferred_element_type=F32)


def _mod_kernel(c_ref, w_ref, b_ref, o_ref):
    c = c_ref[...]
    x = (c * _sigmoid(c)).astype(BF16)
    o_ref[0] = _dot(x, w_ref[0].astype(BF16)) + b_ref[0]


def _mod_call(c_all, w_mod, b_mod):
    depth, d, cols = w_mod.shape
    rows = c_all.shape[0]
    tn = 1536
    return pl.pallas_call(
        _mod_kernel,
        grid=(depth, cols // tn),
        in_specs=[pl.BlockSpec((rows, d), lambda l, j: (0, 0)),
                  pl.BlockSpec((1, d, tn), lambda l, j: (l, 0, j)),
                  pl.BlockSpec((1, 1, tn), lambda l, j: (l, 0, j))],
        out_specs=pl.BlockSpec((1, rows, tn), lambda l, j: (l, 0, j)),
        out_shape=jax.ShapeDtypeStruct((depth, rows, cols), F32),
        compiler_params=_cparams(2),
        name="mod",
    )(c_all, w_mod, b_mod.reshape(depth, 1, cols))


def _ffn_kernel(x_ref, mod_ref, g_ref, wi_ref, wo_ref, o_ref):
    x = x_ref[0]
    shift, scale, gate = mod_ref[0, 0:1, :], mod_ref[0, 1:2, :], mod_ref[0, 2:3, :]
    y = (_rms(x, g_ref[0:1, :]) * (1.0 + scale) + shift).astype(BF16)
    acc = None
    for c in range(D_FF // MXU_TILE):
        lo = c * MXU_TILE
        a = _dot(y, wi_ref[:, lo:lo + MXU_TILE])
        u = _dot(y, wi_ref[:, D_FF + lo:D_FF + lo + MXU_TILE])
        hid = (a * _sigmoid(a) * u).astype(BF16)
        part = _dot(hid, wo_ref[lo:lo + MXU_TILE, :])
        acc = part if acc is None else acc + part
    o_ref[0] = x + 0.5 * gate * _rms(acc, g_ref[1:2, :])


def _ffn_call(x, mod3, g2, wi, wo, tm):
    groups, rows, d = x.shape
    return pl.pallas_call(
        _ffn_kernel,
        grid=(groups, rows // tm),
        in_specs=[pl.BlockSpec((1, tm, d), lambda g, i: (g, i, 0)),
                  pl.BlockSpec((1, 3, d), lambda g, i: (g, 0, 0)),
                  _const_spec(g2.shape), _const_spec(wi.shape), _const_spec(wo.shape)],
        out_specs=pl.BlockSpec((1, tm, d), lambda g, i: (g, i, 0)),
        out_shape=jax.ShapeDtypeStruct(x.shape, F32),
        compiler_params=_cparams(2),
        name="ffn",
    )(x, mod3, g2, wi, wo)


def _swap_half_heads(x):
    lane = lax.broadcasted_iota(jnp.int32, x.shape, 1)
    return jnp.where((lane & (HEAD_DIM - 1)) < HEAD_DIM // 2,
                     pltpu.roll(x, LANES - HEAD_DIM // 2, axis=1),
                     pltpu.roll(x, HEAD_DIM // 2, axis=1))


def _rope_cols(p, cos, sin):
    cols = []
    for j in range(p.shape[1] // LANES):
        x = p[:, j * LANES:(j + 1) * LANES]
        cols.append(x * cos + _swap_half_heads(x) * sin)
    return jnp.concatenate(cols, axis=1) if len(cols) > 1 else cols[0]


def _head_rms_cols(p, gain, ones_bd):
    cols = []
    for j in range(p.shape[1] // LANES):
        x = p[:, j * LANES:(j + 1) * LANES]
        xx = x * x
        hi = xx.astype(BF16)
        lo = (xx - hi.astype(F32)).astype(BF16)
        ms = (_dot(hi, ones_bd) + _dot(lo, ones_bd)) * (1.0 / HEAD_DIM)
        cols.append(x * lax.rsqrt(ms + EPS) * gain)
    return jnp.concatenate(cols, axis=1) if len(cols) > 1 else cols[0]


def _dup_heads(x):
    lane = lax.broadcasted_iota(jnp.int32, x.shape, 1)
    r = pltpu.roll(x, HEAD_DIM, axis=1)
    low = lane < HEAD_DIM
    return jnp.concatenate([jnp.where(low, x, r), jnp.where(low, r, x)], axis=1)


def _proj_kernel(x_ref, mod_ref, g_ref, w_ref, bg_ref, qkg_ref, bd_ref, cos_ref, sin_ref,
                 qa_ref, ka_ref, va_ref, qb_ref, kb_ref, vb_ref, gb_ref, qc_ref, kc_ref, vc_ref, gt_ref,
                 *, rope):
    x = x_ref[0]
    shift, scale = mod_ref[0, 0:1, :], mod_ref[0, 1:2, :]
    y = (_rms(x, g_ref[0:1, :]) * (1.0 + scale) + shift).astype(BF16)
    cos, sin = cos_ref[...], sin_ref[...]

    def proj(name):
        lo, hi = _SPAN[name]
        return _dot(y, w_ref[:, lo:hi])

    def pos(p):
        return _rope_cols(p, cos, sin) if rope else p

    qa_ref[0] = (pos(proj('a_q')) * QK_SCALE).astype(BF16)
    ka_ref[0] = pos(proj('a_k')).astype(BF16)
    va_ref[0] = proj('a_v').astype(BF16)
    qb_ref[0] = pos(proj('b_q')) * QK_SCALE
    kb_ref[0] = pos(proj('b_k'))
    vb_ref[0] = proj('b_v').astype(BF16)
    g = proj('b_g')
    gb_ref[0] = (g * _sigmoid(g)).astype(BF16)
    bd = bd_ref[...]
    qc_ref[0] = (pos(_head_rms_cols(proj('c_q'), qkg_ref[0:1, :], bd)) * QK_SCALE).astype(BF16)
    kc_ref[0] = _dup_heads(pos(_head_rms_cols(proj('c_k'), qkg_ref[1:2, :], bd))).astype(BF16)
    vc_ref[0] = _dup_heads(proj('c_v')).astype(BF16)
    glo = _SPAN['gate'][0]
    for i in range(N_BRANCH):
        lo = glo + i * D_MODEL
        pre = _dot(y, w_ref[:, lo:lo + D_MODEL]) + bg_ref[:, i * D_MODEL:(i + 1) * D_MODEL]
        gt_ref[0, :, i * D_MODEL:(i + 1) * D_MODEL] = _sigmoid(pre).astype(BF16)


_PROJ_OUT = (('qa', 512, BF16), ('ka', 512, BF16), ('va', 512, BF16), ('qb', 256, F32), ('kb', 256, F32),
             ('vb', 512, BF16), ('gb', 512, BF16), ('qc', 512, BF16), ('kc', 256, BF16), ('vc', 256, BF16),
             ('gt', N_BRANCH * D_MODEL, BF16))


def _proj_call(x, mod3, g2, w_in, b_gate, qk_gain, ones_bd, cos, sin, tm, rope):
    groups, rows, d = x.shape
    tok = lambda w: pl.BlockSpec((1, tm, w), lambda g, i: (g, i, 0))
    pos_map = (lambda g, i: (i, 0)) if rope else (lambda g, i: (0, 0))
    outs = pl.pallas_call(
        functools.partial(_proj_kernel, rope=rope),
        grid=(groups, rows // tm),
        in_specs=[tok(d),
                  pl.BlockSpec((1, 3, d), lambda g, i: (g, 0, 0)),
                  _const_spec(g2.shape), _const_spec(w_in.shape), _const_spec(b_gate.shape),
                  _const_spec(qk_gain.shape), _const_spec(ones_bd.shape),
                  pl.BlockSpec((tm, LANES), pos_map), pl.BlockSpec((tm, LANES), pos_map)],
        out_specs=[tok(w) for _, w, _ in _PROJ_OUT],
        out_shape=[jax.ShapeDtypeStruct((groups, rows, w), dt) for _, w, dt in _PROJ_OUT],
        compiler_params=_cparams(2),
        name="proj_rope" if rope else "proj_ctx",
    )(x, mod3, g2, w_in, b_gate, qk_gain, ones_bd, cos, sin)
    return dict(zip([n for n, _, _ in _PROJ_OUT], outs))


def _log_sigmoid(x):
    return jnp.minimum(x, 0.0) - jnp.log(1.0 + jnp.exp(-jnp.abs(x)))


def _split_heads(col):
    lane = lax.broadcasted_iota(jnp.int32, col.shape, 1)
    zero = jnp.zeros_like(col)
    return jnp.concatenate([jnp.where(lane < HEAD_DIM, col, zero),
                            jnp.where(lane >= HEAD_DIM, col, zero)], axis=0)


def _ret_kernel(*refs, n_chunks, with_ctx_out):
    if with_ctx_out:
        (q_ref, k_ref, v_ref, g_ref, kc_ref, vc_ref, qc_ref, gc_ref, logit_ref, gain_ref,
         o_ref, oc_ref) = refs
    else:
        q_ref, k_ref, v_ref, g_ref, kc_ref, vc_ref, logit_ref, gain_ref, o_ref = refs
    C = RET_CHUNK
    di = lax.broadcasted_iota(jnp.int32, (C, C), 0) - lax.broadcasted_iota(jnp.int32, (C, C), 1)
    dpos = jnp.maximum(di, 0).astype(F32)
    dneg = jnp.maximum(-di, 0).astype(F32)
    pos = lax.broadcasted_iota(jnp.int32, (C, 1), 0).astype(F32)
    gain = gain_ref[...]

    tabs = []
    for h in range(RET_HEADS):
        lf = _log_sigmoid(logit_ref[0:1, h:h + 1])
        lb = _log_sigmoid(logit_ref[1:2, h:h + 1])
        tabs.append(dict(
            mask=jnp.where(di >= 0, jnp.exp(lf * dpos), 0.0) + jnp.where(di <= 0, jnp.exp(lb * dneg), 0.0),
            q_f=jnp.exp(lf * (pos + 1.0)), q_b=jnp.exp(lb * (C - pos)),
            k_f=jnp.exp(lf * (C - 1.0 - pos)), k_b=jnp.exp(lb * pos),
            s_f=jnp.exp(lf * C), s_b=jnp.exp(lb * C)))

    def chunk_states(kcol, vfull, h):
        v = vfull[:, h * RET_DV:(h + 1) * RET_DV]
        t = tabs[h]
        return (_dot_tn((kcol * t['k_f']).astype(BF16), v), _dot_tn((kcol * t['k_b']).astype(BF16), v))

    def finish(o, gate):
        return (_rms(o, gain) * gate.astype(F32)).astype(BF16)

    kc_all, vc_all = kc_ref[0], vc_ref[0]
    init = [chunk_states(kc_all[:, (h // 2) * LANES:(h // 2 + 1) * LANES], vc_all, h) for h in range(RET_HEADS)]

    sums = []
    for c in range(n_chunks):
        kc_ = k_ref[0, c * C:(c + 1) * C, :]
        vc_ = v_ref[0, c * C:(c + 1) * C, :]
        sums.append([chunk_states(kc_[:, (h // 2) * LANES:(h // 2 + 1) * LANES], vc_, h) for h in range(RET_HEADS)])
    fwd = [[None] * RET_HEADS for _ in range(n_chunks)]
    bwd = [[None] * RET_HEADS for _ in range(n_chunks)]
    for h in range(RET_HEADS):
        s = init[h][0]
        for c in range(n_chunks):
            fwd[c][h] = s
            s = s * tabs[h]['s_f'] + sums[c][h][0]
        s = init[h][1]
        for c in reversed(range(n_chunks)):
            bwd[c][h] = s
            s = s * tabs[h]['s_b'] + sums[c][h][1]

    def chunk_out(q, k, v, states):
        outs = []
        for j in range(RET_HEADS // 2):
            qs = _split_heads(q[:, j * LANES:(j + 1) * LANES])
            s = _dot_nt(qs.astype(BF16), k[:, j * LANES:(j + 1) * LANES].astype(BF16))
            for hh in range(2):
                h = 2 * j + hh
                t = tabs[h]
                att = (s[hh * C:(hh + 1) * C] * t['mask']).astype(BF16)
                o = _dot(att, v[:, h * RET_DV:(h + 1) * RET_DV])
                if states is not None:
                    qm = qs[hh * C:(hh + 1) * C]
                    sf, sb = states[h]
                    o = o + _dot((qm * t['q_f']).astype(BF16), sf.astype(BF16))
                    o = o + _dot((qm * t['q_b']).astype(BF16), sb.astype(BF16))
                outs.append(o)
        return outs

    for c in range(n_chunks):
        rows = slice(c * C, (c + 1) * C)
        outs = chunk_out(q_ref[0, rows, :], k_ref[0, rows, :], v_ref[0, rows, :],
                         [(fwd[c][h], bwd[c][h]) for h in range(RET_HEADS)])
        for h in range(RET_HEADS):
            cols = slice(h * RET_DV, (h + 1) * RET_DV)
            o_ref[0, rows, cols] = finish(outs[h], g_ref[0, rows, cols])

    if with_ctx_out:
        outs = chunk_out(qc_ref[0], kc_all, vc_all, None)
        for h in range(RET_HEADS):
            cols = slice(h * RET_DV, (h + 1) * RET_DV)
            oc_ref[0, :, cols] = finish(outs[h], gc_ref[0, :, cols])


def _ret_call(pl_, pc_, logit, gain, with_ctx_out):
    b, n, _ = pl_['qb'].shape
    lc = pc_['kb'].shape[1]
    assert lc == RET_CHUNK and n % RET_CHUNK == 0
    full = lambda a: pl.BlockSpec((1,) + a.shape[1:], lambda i: (i, 0, 0))
    ins = [pl_['qb'], pl_['kb'], pl_['vb'], pl_['gb'], pc_['kb'], pc_['vb']]
    if with_ctx_out:
        ins += [pc_['qb'], pc_['gb']]
    in_specs = [full(a) for a in ins] + [_const_spec(logit.shape), _const_spec(gain.shape)]
    out_shape = [jax.ShapeDtypeStruct((b, n, RET_HEADS * RET_DV), BF16)]
    if with_ctx_out:
        out_shape.append(jax.ShapeDtypeStruct((b, lc, RET_HEADS * RET_DV), BF16))
    outs = pl.pallas_call(
        functools.partial(_ret_kernel, n_chunks=n // RET_CHUNK, with_ctx_out=with_ctx_out),
        grid=(b,),
        in_specs=in_specs,
        out_specs=[pl.BlockSpec((1,) + s.shape[1:], lambda i: (i, 0, 0)) for s in out_shape],
        out_shape=out_shape,
        compiler_params=_cparams(1),
        name="retention_ctx" if with_ctx_out else "retention",
    )(*ins, logit, gain)
    return outs if with_ctx_out else (outs[0], None)


def _attn_col(qcol, kv):
    tq = qcol.shape[0]
    qs = _split_heads(qcol)
    scores = [_dot_nt(qs, k) for k, _ in kv]
    m = functools.reduce(jnp.maximum, [jnp.max(s, axis=-1, keepdims=True) for s in scores])
    es = [jnp.exp(s - m) for s in scores]
    denom = functools.reduce(lambda a, b: a + b, [jnp.sum(e, axis=-1, keepdims=True) for e in es])
    o = functools.reduce(lambda a, b: a + b, [_dot(e.astype(BF16), v) for e, (_, v) in zip(es, kv)])
    o = o * (1.0 / denom)
    return o[:tq], o[tq:]


def _attn_kernel(*refs, has_lat, lam_init):
    if has_lat:
        (qa_ref, qc_ref, ka_ref, va_ref, kc_ref, vc_ref, kac_ref, vac_ref, kcc_ref, vcc_ref,
         dl_ref, dg_ref, oa_ref, oc_ref) = refs
    else:
        qa_ref, qc_ref, kac_ref, vac_ref, kcc_ref, vcc_ref, dl_ref, dg_ref, oa_ref, oc_ref = refs
    dl = dl_ref[...]
    lam = (jnp.exp(jnp.sum(dl[0:1] * dl[1:2], axis=-1, keepdims=True))
           - jnp.exp(jnp.sum(dl[2:3] * dl[3:4], axis=-1, keepdims=True)) + lam_init)
    dgain = dg_ref[...] * (1.0 - lam_init)
    lane = lax.broadcasted_iota(jnp.int32, (qa_ref.shape[1], LANES), 1)

    for h in range(DA_HEADS):
        cols = slice(h * LANES, (h + 1) * LANES)
        kv = [(kac_ref[0, :, cols], vac_ref[0, :, cols])]
        if has_lat:
            kv = [(ka_ref[0, :, cols], va_ref[0, :, cols])] + kv
        o1, o2 = _attn_col(qa_ref[0, :, cols], kv)
        oa_ref[0, :, cols] = (_rms(o1 - lam * o2, 1.0) * dgain).astype(BF16)

    for j in range(GQA_HEADS // 2):
        cols = slice(j * LANES, (j + 1) * LANES)
        gcols = slice((j // 2) * LANES, (j // 2 + 1) * LANES)
        kv = [(kcc_ref[0, :, gcols], vcc_ref[0, :, gcols])]
        if has_lat:
            kv = [(kc_ref[0, :, gcols], vc_ref[0, :, gcols])] + kv
        o1, o2 = _attn_col(qc_ref[0, :, cols], kv)
        oc_ref[0, :, cols] = jnp.where(lane < HEAD_DIM, o1, o2).astype(BF16)


def _attn_call(pq, pl_, pc_, diff_lambda, diff_gain, lam_init, tq):
    b, n, _ = pq['qa'].shape
    has_lat = pl_ is not None
    qspec = lambda w: pl.BlockSpec((1, tq, w), lambda i, j: (i, j, 0))
    bspec = lambda a: pl.BlockSpec((1,) + a.shape[1:], lambda i, j: (i, 0, 0))
    ins = [pq['qa'], pq['qc']]
    in_specs = [qspec(512), qspec(512)]
    if has_lat:
        kvl = [pl_['ka'], pl_['va'], pl_['kc'], pl_['vc']]
        ins += kvl
        in_specs += [bspec(a) for a in kvl]
    kvc = [pc_['ka'], pc_['va'], pc_['kc'], pc_['vc']]
    ins += kvc + [diff_lambda, diff_gain]
    in_specs += [bspec(a) for a in kvc] + [_const_spec(diff_lambda.shape), _const_spec(diff_gain.shape)]
    out_shape = [jax.ShapeDtypeStruct((b, n, BRANCH_W), BF16)] * 2
    return pl.pallas_call(
        functools.partial(_attn_kernel, has_lat=has_lat, lam_init=lam_init),
        grid=(b, n // tq),
        in_specs=in_specs,
        out_specs=[qspec(BRANCH_W), qspec(BRANCH_W)],
        out_shape=out_shape,
        compiler_params=_cparams(2),
        name="attn_lat" if has_lat else "attn_ctx",
    )(*ins)


def _merge_kernel(h_ref, mod_ref, g_ref, oa_ref, ob_ref, oc_ref, gt_ref, wb_ref, wo_ref, o_ref):
    merged = None
    for i, br in enumerate((oa_ref, ob_ref, oc_ref)):
        p = _dot(br[0], wb_ref[i]) * gt_ref[0, :, i * D_MODEL:(i + 1) * D_MODEL].astype(F32)
        merged = p if merged is None else merged + p
    out = _dot(merged.astype(BF16), wo_ref[...])
    o_ref[0] = h_ref[0] + mod_ref[0, 2:3, :] * _rms(out, g_ref[1:2, :])


def _merge_call(h, mod3, g2, oa, ob, oc, gt, w_branch, w_out, tm):
    groups, rows, d = h.shape
    tok = lambda w: pl.BlockSpec((1, tm, w), lambda g, i: (g, i, 0))
    return pl.pallas_call(
        _merge_kernel,
        grid=(groups, rows // tm),
        in_specs=[tok(d), pl.BlockSpec((1, 3, d), lambda g, i: (g, 0, 0)), _const_spec(g2.shape),
                  tok(BRANCH_W), tok(BRANCH_W), tok(BRANCH_W), tok(N_BRANCH * d),
                  _const_spec(w_branch.shape), _const_spec(w_out.shape)],
        out_specs=tok(d),
        out_shape=jax.ShapeDtypeStruct(h.shape, F32),
        compiler_params=_cparams(2),
        name="merge",
    )(h, mod3, g2, oa, ob, oc, gt, w_branch, w_out)


def _rope_tables(n_tok):
    t = np.arange(n_tok)
    n_freq = ROPE_DIM // 4
    inv = ROPE_THETA ** (-np.arange(n_freq, dtype=np.float64) / n_freq)
    ang = np.concatenate([(t // GRID_W)[:, None] * inv, (t % GRID_W)[:, None] * inv], axis=-1)
    cos, sin = np.cos(ang), np.sin(ang)
    cos128 = np.tile(cos, (1, LANES // (ROPE_DIM // 2)))
    sin128 = np.tile(np.concatenate([-sin, sin], axis=-1), (1, LANES // ROPE_DIM))
    return jnp.asarray(cos128, F32), jnp.asarray(sin128, F32)


def _ones_block_diag():
    i = np.arange(LANES)
    return jnp.asarray((i[:, None] // HEAD_DIM) == (i[None, :] // HEAD_DIM), BF16)


def kernel(x, c, ctx, c_ctx, w_mod, b_mod, norm_g, w_ffn_in, w_ffn_out, w_in, b_gate, diff_lambda,
           diff_norm_g, ret_decay_logit, ret_norm_g, qk_norm_g, w_branch, w_out):
    batch, seq, d = x.shape
    ctx_len = ctx.shape[1]
    depth = w_mod.shape[0]
    assert d == D_MODEL and seq % 512 == 0 and (batch * ctx_len) % 512 == 0

    pad = (-(batch + 1)) % 8
    c_all = jnp.concatenate([c, c_ctx[None, :], jnp.zeros((pad, d), F32)], axis=0)
    mod_all = _mod_call(c_all, w_mod, b_mod).reshape(depth, batch + 1 + pad, N_SUB, 3, d)

    wi = w_ffn_in.astype(BF16)
    wo = w_ffn_out.astype(BF16)
    w_in_b = w_in.astype(BF16)
    wb = w_branch.astype(BF16)
    wout = w_out.astype(BF16)
    cos, sin = _rope_tables(seq)
    ones_bd = _ones_block_diag()
    qk_gain = jnp.tile(qk_norm_g, (1, 1, LANES // HEAD_DIM))

    h = x
    hc = ctx.reshape(1, batch * ctx_len, d)
    for l in range(depth):
        last = l == depth - 1
        lam_init = 0.8 - 0.6 * math.exp(-0.3 * l)
        mod_l = lambda s: mod_all[l, :batch, s]
        mod_c = lambda s: mod_all[l, batch:batch + 1, s]
        ng = norm_g[l]

        h = _ffn_call(h, mod_l(0), ng[0:2], wi[l, 0], wo[l, 0], 512)
        hc = _ffn_call(hc, mod_c(0), ng[0:2], wi[l, 0], wo[l, 0], 512)

        bg = b_gate[l][None, :]
        p_lat = _proj_call(h, mod_l(1), ng[2:4], w_in_b[l], bg, qk_gain[l], ones_bd, cos, sin, 512, True)
        p_ctx = _proj_call(hc, mod_c(1), ng[2:4], w_in_b[l], bg, qk_gain[l], ones_bd, cos, sin, 512, False)
        p_ctx = {k: v.reshape(batch, ctx_len, v.shape[-1]) for k, v in p_ctx.items()}

        ob, ob_c = _ret_call(p_lat, p_ctx, ret_decay_logit[l], ret_norm_g[l][None, :], not last)
        dgain = diff_norm_g[l][None, :]
        oa, oc = _attn_call(p_lat, p_lat, p_ctx, diff_lambda[l], dgain, lam_init, 256)
        h = _merge_call(h, mod_l(1), ng[2:4], oa, ob, oc, p_lat['gt'], wb[l], wout[l], 512)
        if not last:
            oa_c, oc_c = _attn_call(p_ctx, None, p_ctx, diff_lambda[l], dgain, lam_init, ctx_len)
            flat = lambda a: a.reshape(1, batch * ctx_len, a.shape[-1])
            hc = _merge_call(hc, mod_c(1), ng[2:4], flat(oa_c), flat(ob_c), flat(oc_c), flat(p_ctx['gt']),
                             wb[l], wout[l], 512)

        h = _ffn_call(h, mod_l(2), ng[4:6], wi[l, 1], wo[l, 1], 512)
        if not last:
            hc = _ffn_call(hc, mod_c(2), ng[4:6], wi[l, 1], wo[l, 1], 512)
    return h
```

```python
import functools
import math

import numpy as np
import jax
import jax.numpy as jnp
from jax import lax
from jax.experimental import pallas as pl
from jax.experimental.pallas import tpu as pltpu

F32 = jnp.float32
BF16 = jnp.bfloat16

D_MODEL = 1024
N_SUB = 3
D_FF = 2816
EPS = 1e-6
ROPE_THETA = 10000.0
ROPE_DIM = 64
GRID_W = 64
HEAD_DIM = 64
LANES = 128
DA_HEADS = 4
RET_HEADS = 4
RET_DV = 128
GQA_HEADS = 8
GQA_KV = 2
N_BRANCH = 3
BRANCH_W = 512
QK_SCALE = HEAD_DIM ** -0.5
SOFTMAX_Q_SCALE = QK_SCALE * math.log2(math.e)
MXU_TILE = 256
RET_CHUNK = MXU_TILE
ATTN_KEY_BLOCK = MXU_TILE
ONES_ROWS = 16
DA_VROWS = LANES + ONES_ROWS
GQA_VROWS = HEAD_DIM + ONES_ROWS
VMEM_LIMIT = 56 * 1024 * 1024

_SIZES = (512, 512, 512, 256, 256, 512, 512, 512, 128, 128, N_BRANCH * D_MODEL)
_NAMES = ('a_q', 'a_k', 'a_v', 'b_q', 'b_k', 'b_v', 'b_g', 'c_q', 'c_k', 'c_v', 'gate')
_SPAN = {}
_lo = 0
for _n, _s in zip(_NAMES, _SIZES):
    _SPAN[_n] = (_lo, _lo + _s)
    _lo += _s
IN_COLS = _lo


def _cparams(n_axes):
    return pltpu.CompilerParams(dimension_semantics=("arbitrary",) * n_axes,
                                vmem_limit_bytes=VMEM_LIMIT)


def _const_spec(shape):
    nd = len(shape)
    return pl.BlockSpec(shape, lambda *_: (0,) * nd, pipeline_mode=pl.Buffered(1))


def _rms(x, g):
    return x * lax.rsqrt(jnp.mean(x * x, axis=-1, keepdims=True) + EPS) * g


def _sigmoid(x):
    return 1.0 / (1.0 + jnp.exp(-x))


def _dot(a, b):
    return jnp.dot(a, b, preferred_element_type=F32)


def _dot_nt(a, b):
    return lax.dot_general(a, b, (((1,), (1,)), ((), ())), preferred_element_type=F32)


def _dot_tn(a, b):
    return lax.dot_general(a, b, (((0,), (0,)), ((), ())), preferred_element_type=F32)


def _mod_kernel(c_ref, w_ref, b_ref, o_ref):
    c = c_ref[...]
    x = (c * _sigmoid(c)).astype(BF16)
    o_ref[0] = _dot(x, w_ref[0].astype(BF16)) + b_ref[0]


def _mod_call(c_all, w_mod, b_mod):
    depth, d, cols = w_mod.shape
    rows = c_all.shape[0]
    tn = 1536
    return pl.pallas_call(
        _mod_kernel,
        grid=(depth, cols // tn),
        in_specs=[pl.BlockSpec((rows, d), lambda l, j: (0, 0)),
                  pl.BlockSpec((1, d, tn), lambda l, j: (l, 0, j)),
                  pl.BlockSpec((1, 1, tn), lambda l, j: (l, 0, j))],
        out_specs=pl.BlockSpec((1, rows, tn), lambda l, j: (l, 0, j)),
        out_shape=jax.ShapeDtypeStruct((depth, rows, cols), F32),
        compiler_params=_cparams(2),
        name="mod",
    )(c_all, w_mod, b_mod.reshape(depth, 1, cols))


def _ffn_kernel(x_ref, mod_ref, g_ref, wi_ref, wo_ref, o_ref):
    x = x_ref[0]
    shift, scale, gate = mod_ref[0, 0:1, :], mod_ref[0, 1:2, :], mod_ref[0, 2:3, :]
    y = (_rms(x, g_ref[0:1, :]) * (1.0 + scale) + shift).astype(BF16)
    acc = None
    for c in range(D_FF // MXU_TILE):
        lo = c * MXU_TILE
        a = _dot(y, wi_ref[:, lo:lo + MXU_TILE])
        u = _dot(y, wi_ref[:, D_FF + lo:D_FF + lo + MXU_TILE])
        hid = (a * _sigmoid(a) * u).astype(BF16)
        part = _dot(hid, wo_ref[lo:lo + MXU_TILE, :])
        acc = part if acc is None else acc + part
    o_ref[0] = x + 0.5 * gate * _rms(acc, g_ref[1:2, :])


def _ffn_call(x, mod3, g2, wi, wo, tm):
    groups, rows, d = x.shape
    return pl.pallas_call(
        _ffn_kernel,
        grid=(groups, rows // tm),
        in_specs=[pl.BlockSpec((1, tm, d), lambda g, i: (g, i, 0)),
                  pl.BlockSpec((1, 3, d), lambda g, i: (g, 0, 0)),
                  _const_spec(g2.shape), _const_spec(wi.shape), _const_spec(wo.shape)],
        out_specs=pl.BlockSpec((1, tm, d), lambda g, i: (g, i, 0)),
        out_shape=jax.ShapeDtypeStruct(x.shape, F32),
        compiler_params=_cparams(2),
        name="ffn",
    )(x, mod3, g2, wi, wo)


def _swap_half_heads(x):
    lane = lax.broadcasted_iota(jnp.int32, x.shape, 1)
    return jnp.where((lane & (HEAD_DIM - 1)) < HEAD_DIM // 2,
                     pltpu.roll(x, LANES - HEAD_DIM // 2, axis=1),
                     pltpu.roll(x, HEAD_DIM // 2, axis=1))


def _rope_cols(p, cos, sin):
    cols = []
    for j in range(p.shape[1] // LANES):
        x = p[:, j * LANES:(j + 1) * LANES]
        cols.append(x * cos + _swap_half_heads(x) * sin)
    return jnp.concatenate(cols, axis=1) if len(cols) > 1 else cols[0]


def _head_rms_cols(p, gain, ones_bd):
    cols = []
    for j in range(p.shape[1] // LANES):
        x = p[:, j * LANES:(j + 1) * LANES]
        xx = x * x
        hi = xx.astype(BF16)
        lo = (xx - hi.astype(F32)).astype(BF16)
        ms = (_dot(hi, ones_bd) + _dot(lo, ones_bd)) * (1.0 / HEAD_DIM)
        cols.append(x * lax.rsqrt(ms + EPS) * gain)
    return jnp.concatenate(cols, axis=1) if len(cols) > 1 else cols[0]


def _dup_heads(x):
    lane = lax.broadcasted_iota(jnp.int32, x.shape, 1)
    r = pltpu.roll(x, HEAD_DIM, axis=1)
    low = lane < HEAD_DIM
    return jnp.concatenate([jnp.where(low, x, r), jnp.where(low, r, x)], axis=1)


def _proj_kernel(x_ref, mod_ref, g_ref, w_ref, bg_ref, qkg_ref, bd_ref, cos_ref, sin_ref,
                 qa_ref, ka_ref, va_ref, qb_ref, kb_ref, vb_ref, gb_ref, qc_ref, kc_ref, vc_ref, gt_ref,
                 *, rope):
    x = x_ref[0]
    shift, scale = mod_ref[0, 0:1, :], mod_ref[0, 1:2, :]
    y = (_rms(x, g_ref[0:1, :]) * (1.0 + scale) + shift).astype(BF16)
    cos, sin = cos_ref[...], sin_ref[...]

    def proj(name):
        lo, hi = _SPAN[name]
        return _dot(y, w_ref[:, lo:hi])

    def pos(p):
        return _rope_cols(p, cos, sin) if rope else p

    qa_ref[0] = (pos(proj('a_q')) * SOFTMAX_Q_SCALE).astype(BF16)
    ka_ref[0] = pos(proj('a_k')).astype(BF16)
    va_ref[0] = proj('a_v').astype(BF16)
    qb_ref[0] = pos(proj('b_q')) * QK_SCALE
    kb_ref[0] = pos(proj('b_k'))
    vb_ref[0] = proj('b_v').astype(BF16)
    g = proj('b_g')
    gb_ref[0] = (g * _sigmoid(g)).astype(BF16)
    bd = bd_ref[...]
    qc_ref[0] = (pos(_head_rms_cols(proj('c_q'), qkg_ref[0:1, :], bd)) * SOFTMAX_Q_SCALE).astype(BF16)
    kc_ref[0] = _dup_heads(pos(_head_rms_cols(proj('c_k'), qkg_ref[1:2, :], bd))).astype(BF16)
    vc_ref[0] = proj('c_v').astype(BF16)
    glo = _SPAN['gate'][0]
    for i in range(N_BRANCH):
        lo = glo + i * D_MODEL
        pre = _dot(y, w_ref[:, lo:lo + D_MODEL]) + bg_ref[:, i * D_MODEL:(i + 1) * D_MODEL]
        gt_ref[0, :, i * D_MODEL:(i + 1) * D_MODEL] = _sigmoid(pre).astype(BF16)


_PROJ_OUT = (('qa', 512, BF16), ('ka', 512, BF16), ('va', 512, BF16), ('qb', 256, F32), ('kb', 256, F32),
             ('vb', 512, BF16), ('gb', 512, BF16), ('qc', 512, BF16), ('kc', 256, BF16), ('vc', 128, BF16),
             ('gt', N_BRANCH * D_MODEL, BF16))


def _proj_call(x, mod3, g2, w_in, b_gate, qk_gain, ones_bd, cos, sin, tm, rope):
    groups, rows, d = x.shape
    tok = lambda w: pl.BlockSpec((1, tm, w), lambda g, i: (g, i, 0))
    pos_map = (lambda g, i: (i, 0)) if rope else (lambda g, i: (0, 0))
    outs = pl.pallas_call(
        functools.partial(_proj_kernel, rope=rope),
        grid=(groups, rows // tm),
        in_specs=[tok(d),
                  pl.BlockSpec((1, 3, d), lambda g, i: (g, 0, 0)),
                  _const_spec(g2.shape), _const_spec(w_in.shape), _const_spec(b_gate.shape),
                  _const_spec(qk_gain.shape), _const_spec(ones_bd.shape),
                  pl.BlockSpec((tm, LANES), pos_map), pl.BlockSpec((tm, LANES), pos_map)],
        out_specs=[tok(w) for _, w, _ in _PROJ_OUT],
        out_shape=[jax.ShapeDtypeStruct((groups, rows, w), dt) for _, w, dt in _PROJ_OUT],
        compiler_params=_cparams(2),
        name="proj_rope" if rope else "proj_ctx",
    )(x, mod3, g2, w_in, b_gate, qk_gain, ones_bd, cos, sin)
    return dict(zip([n for n, _, _ in _PROJ_OUT], outs))


def _log_sigmoid(x):
    return jnp.minimum(x, 0.0) - jnp.log(1.0 + jnp.exp(-jnp.abs(x)))


def _split_heads(col):
    lane = lax.broadcasted_iota(jnp.int32, col.shape, 1)
    zero = jnp.zeros_like(col)
    return jnp.concatenate([jnp.where(lane < HEAD_DIM, col, zero),
                            jnp.where(lane >= HEAD_DIM, col, zero)], axis=0)


def _ret_kernel(*refs, n_chunks, with_ctx_out):
    if with_ctx_out:
        (q_ref, k_ref, v_ref, g_ref, kc_ref, vc_ref, qc_ref, gc_ref, logit_ref, gain_ref,
         o_ref, oc_ref) = refs
    else:
        q_ref, k_ref, v_ref, g_ref, kc_ref, vc_ref, logit_ref, gain_ref, o_ref = refs
    C = RET_CHUNK
    di = lax.broadcasted_iota(jnp.int32, (C, C), 0) - lax.broadcasted_iota(jnp.int32, (C, C), 1)
    dpos = jnp.maximum(di, 0).astype(F32)
    dneg = jnp.maximum(-di, 0).astype(F32)
    pos = lax.broadcasted_iota(jnp.int32, (C, 1), 0).astype(F32)
    gain = gain_ref[...]

    tabs = []
    for h in range(RET_HEADS):
        lf = _log_sigmoid(logit_ref[0:1, h:h + 1])
        lb = _log_sigmoid(logit_ref[1:2, h:h + 1])
        tabs.append(dict(
            mask=jnp.where(di >= 0, jnp.exp(lf * dpos), 0.0) + jnp.where(di <= 0, jnp.exp(lb * dneg), 0.0),
            q_f=jnp.exp(lf * (pos + 1.0)), q_b=jnp.exp(lb * (C - pos)),
            k_f=jnp.exp(lf * (C - 1.0 - pos)), k_b=jnp.exp(lb * pos),
            s_f=jnp.exp(lf * C), s_b=jnp.exp(lb * C)))

    def chunk_states(kcol, vfull, h):
        v = vfull[:, h * RET_DV:(h + 1) * RET_DV]
        t = tabs[h]
        return (_dot_tn((kcol * t['k_f']).astype(BF16), v), _dot_tn((kcol * t['k_b']).astype(BF16), v))

    def finish(o, gate):
        return (_rms(o, gain) * gate.astype(F32)).astype(BF16)

    kc_all, vc_all = kc_ref[0], vc_ref[0]
    init = [chunk_states(kc_all[:, (h // 2) * LANES:(h // 2 + 1) * LANES], vc_all, h) for h in range(RET_HEADS)]

    sums = []
    for c in range(n_chunks):
        kc_ = k_ref[0, c * C:(c + 1) * C, :]
        vc_ = v_ref[0, c * C:(c + 1) * C, :]
        sums.append([chunk_states(kc_[:, (h // 2) * LANES:(h // 2 + 1) * LANES], vc_, h) for h in range(RET_HEADS)])
    fwd = [[None] * RET_HEADS for _ in range(n_chunks)]
    bwd = [[None] * RET_HEADS for _ in range(n_chunks)]
    for h in range(RET_HEADS):
        s = init[h][0]
        for c in range(n_chunks):
            fwd[c][h] = s
            s = s * tabs[h]['s_f'] + sums[c][h][0]
        s = init[h][1]
        for c in reversed(range(n_chunks)):
            bwd[c][h] = s
            s = s * tabs[h]['s_b'] + sums[c][h][1]

    def chunk_out(q, k, v, states):
        outs = []
        for j in range(RET_HEADS // 2):
            qs = _split_heads(q[:, j * LANES:(j + 1) * LANES])
            s = _dot_nt(qs.astype(BF16), k[:, j * LANES:(j + 1) * LANES].astype(BF16))
            for hh in range(2):
                h = 2 * j + hh
                t = tabs[h]
                att = (s[hh * C:(hh + 1) * C] * t['mask']).astype(BF16)
                o = _dot(att, v[:, h * RET_DV:(h + 1) * RET_DV])
                if states is not None:
                    qm = qs[hh * C:(hh + 1) * C]
                    sf, sb = states[h]
                    o = o + _dot((qm * t['q_f']).astype(BF16), sf.astype(BF16))
                    o = o + _dot((qm * t['q_b']).astype(BF16), sb.astype(BF16))
                outs.append(o)
        return outs

    for c in range(n_chunks):
        rows = slice(c * C, (c + 1) * C)
        outs = chunk_out(q_ref[0, rows, :], k_ref[0, rows, :], v_ref[0, rows, :],
                         [(fwd[c][h], bwd[c][h]) for h in range(RET_HEADS)])
        for h in range(RET_HEADS):
            cols = slice(h * RET_DV, (h + 1) * RET_DV)
            o_ref[0, rows, cols] = finish(outs[h], g_ref[0, rows, cols])

    if with_ctx_out:
        outs = chunk_out(qc_ref[0], kc_all, vc_all, None)
        for h in range(RET_HEADS):
            cols = slice(h * RET_DV, (h + 1) * RET_DV)
            oc_ref[0, :, cols] = finish(outs[h], gc_ref[0, :, cols])


def _ret_call(pl_, pc_, logit, gain, with_ctx_out):
    b, n, _ = pl_['qb'].shape
    lc = pc_['kb'].shape[1]
    assert lc == RET_CHUNK and n % RET_CHUNK == 0
    full = lambda a: pl.BlockSpec((1,) + a.shape[1:], lambda i: (i, 0, 0))
    ins = [pl_['qb'], pl_['kb'], pl_['vb'], pl_['gb'], pc_['kb'], pc_['vb']]
    if with_ctx_out:
        ins += [pc_['qb'], pc_['gb']]
    in_specs = [full(a) for a in ins] + [_const_spec(logit.shape), _const_spec(gain.shape)]
    out_shape = [jax.ShapeDtypeStruct((b, n, RET_HEADS * RET_DV), BF16)]
    if with_ctx_out:
        out_shape.append(jax.ShapeDtypeStruct((b, lc, RET_HEADS * RET_DV), BF16))
    outs = pl.pallas_call(
        functools.partial(_ret_kernel, n_chunks=n // RET_CHUNK, with_ctx_out=with_ctx_out),
        grid=(b,),
        in_specs=in_specs,
        out_specs=[pl.BlockSpec((1,) + s.shape[1:], lambda i: (i, 0, 0)) for s in out_shape],
        out_shape=out_shape,
        compiler_params=_cparams(1),
        name="retention_ctx" if with_ctx_out else "retention",
    )(*ins, logit, gain)
    return outs if with_ctx_out else (outs[0], None)


def _fold_rows(x, op):
    r, n = x.shape
    return op(x.reshape(r // 8, 8, n), axis=0)


def _attn_kernel(*refs, has_lat, lam_init):
    if has_lat:
        (qa_ref, qc_ref, ka_ref, va_ref, kc_ref, vc_ref, kac_ref, vac_ref, kcc_ref, vcc_ref,
         dl_ref, dg_ref, oa_ref, oc_ref, vat_ref, vct_ref, vact_ref, vcct_ref, s_ref) = refs
    else:
        (qa_ref, qc_ref, kac_ref, vac_ref, kcc_ref, vcc_ref, dl_ref, dg_ref, oa_ref, oc_ref,
         vact_ref, vcct_ref, s_ref) = refs

    @pl.when(pl.program_id(1) == 0)
    def _():
        pairs = [(vac_ref, vact_ref, vcc_ref, vcct_ref)]
        if has_lat:
            pairs.append((va_ref, vat_ref, vc_ref, vct_ref))
        for va_src, va_dst, vc_src, vc_dst in pairs:
            ones = jnp.ones((ONES_ROWS, va_src.shape[1]), BF16)
            for c in range(DA_HEADS):
                va_dst[c * DA_VROWS:c * DA_VROWS + LANES, :] = (
                    va_src[0, :, c * LANES:(c + 1) * LANES].astype(F32).T.astype(BF16))
                va_dst[c * DA_VROWS + LANES:(c + 1) * DA_VROWS, :] = ones
            vt = vc_src[0].astype(F32).T.astype(BF16)
            for g in range(GQA_KV):
                vc_dst[g * GQA_VROWS:g * GQA_VROWS + HEAD_DIM, :] = vt[g * HEAD_DIM:(g + 1) * HEAD_DIM]
                vc_dst[g * GQA_VROWS + HEAD_DIM:(g + 1) * GQA_VROWS, :] = ones

    dl = dl_ref[...]
    lam = (jnp.exp(jnp.sum(dl[0:1] * dl[1:2], axis=-1, keepdims=True))
           - jnp.exp(jnp.sum(dl[2:3] * dl[3:4], axis=-1, keepdims=True)) + lam_init)
    dgain = dg_ref[...] * (1.0 - lam_init)
    tq = qa_ref.shape[1]
    n_cols = DA_HEADS + GQA_HEADS // 2
    t_lat = ka_ref.shape[1] if has_lat else 0
    t_ctx = kac_ref.shape[1]
    kb = ATTN_KEY_BLOCK
    assert t_lat % kb == 0 and t_ctx % kb == 0
    blocks = [(False, r) for r in range(0, t_lat, kb)] + [(True, r) for r in range(0, t_ctx, kb)]

    def column(c):
        if c < DA_HEADS:
            cols = slice(c * LANES, (c + 1) * LANES)
            return (qa_ref[0, :, cols], (ka_ref if has_lat else None, kac_ref), cols,
                    (vat_ref if has_lat else None, vact_ref), slice(c * DA_VROWS, (c + 1) * DA_VROWS), LANES)
        j = c - DA_HEADS
        g = j // 2
        return (qc_ref[0, :, slice(j * LANES, (j + 1) * LANES)], (kc_ref if has_lat else None, kcc_ref),
                slice(g * LANES, (g + 1) * LANES),
                (vct_ref if has_lat else None, vcct_ref), slice(g * GQA_VROWS, (g + 1) * GQA_VROWS), HEAD_DIM)

    runtime_zero = jnp.minimum(pl.program_id(1), 0)

    def score_block(c, qs, i):
        _, krefs, kcols, _, _, _ = column(c)
        is_ctx, r = blocks[i]
        s = _dot_nt(krefs[is_ctx][0, r:r + kb, kcols], qs)
        s_ref[c % 2 + runtime_zero, i * kb:(i + 1) * kb, :] = s
        return _fold_rows(s, jnp.max)

    def value_block(c, m, i):
        _, _, _, vrefs, vrows, _ = column(c)
        is_ctx, r = blocks[i]
        e = jnp.exp2(s_ref[c % 2 + runtime_zero, i * kb:(i + 1) * kb, :] - m)
        return _dot(vrefs[is_ctx][vrows, r:r + kb], e.astype(BF16))

    def finish(c, o_ext):
        dv = column(c)[5]
        o = o_ext[:dv] * (1.0 / o_ext[dv:dv + 1])
        o1, o2 = o[:, :tq], o[:, tq:]
        if c < DA_HEADS:
            d = o1 - lam * o2
            d = d * lax.rsqrt(jnp.mean(d * d, axis=0, keepdims=True) + EPS)
            oa_ref[0, :, c * LANES:(c + 1) * LANES] = (d.T * dgain).astype(BF16)
        else:
            j = c - DA_HEADS
            oc_ref[0, :, j * LANES:(j + 1) * LANES] = jnp.concatenate([o1, o2], axis=0).T.astype(BF16)

    def tree(op, xs):
        return functools.reduce(op, xs)

    qs = _split_heads(column(0)[0])
    m = jnp.max(tree(jnp.maximum, [score_block(0, qs, i) for i in range(len(blocks))]), axis=0, keepdims=True)
    for c in range(n_cols):
        nxt = c + 1 < n_cols
        if nxt:
            qs_next = _split_heads(column(c + 1)[0])
        maxes, outs = [], []
        for i in range(len(blocks)):
            if nxt:
                maxes.append(score_block(c + 1, qs_next, i))
            outs.append(value_block(c, m, i))
        finish(c, tree(lambda a, b: a + b, outs))
        if nxt:
            m = jnp.max(tree(jnp.maximum, maxes), axis=0, keepdims=True)


def _attn_call(pq, pl_, pc_, diff_lambda, diff_gain, lam_init, tq):
    b, n, _ = pq['qa'].shape
    has_lat = pl_ is not None
    qspec = lambda w: pl.BlockSpec((1, tq, w), lambda i, j: (i, j, 0))
    bspec = lambda a: pl.BlockSpec((1,) + a.shape[1:], lambda i, j: (i, 0, 0))
    ins = [pq['qa'], pq['qc']]
    in_specs = [qspec(512), qspec(512)]
    if has_lat:
        kvl = [pl_['ka'], pl_['va'], pl_['kc'], pl_['vc']]
        ins += kvl
        in_specs += [bspec(a) for a in kvl]
    kvc = [pc_['ka'], pc_['va'], pc_['kc'], pc_['vc']]
    ins += kvc + [diff_lambda, diff_gain]
    in_specs += [bspec(a) for a in kvc] + [_const_spec(diff_lambda.shape), _const_spec(diff_gain.shape)]
    out_shape = [jax.ShapeDtypeStruct((b, n, BRANCH_W), BF16)] * 2
    transposed = lambda p: [pltpu.VMEM((DA_HEADS * DA_VROWS, p['va'].shape[1]), BF16),
                            pltpu.VMEM((GQA_KV * GQA_VROWS, p['vc'].shape[1]), BF16)]
    scratch = (transposed(pl_) if has_lat else []) + transposed(pc_)
    t_total = pc_['ka'].shape[1] + (pl_['ka'].shape[1] if has_lat else 0)
    scratch.append(pltpu.VMEM((2, t_total, 2 * tq), F32))
    return pl.pallas_call(
        functools.partial(_attn_kernel, has_lat=has_lat, lam_init=lam_init),
        grid=(b, n // tq),
        in_specs=in_specs,
        out_specs=[qspec(BRANCH_W), qspec(BRANCH_W)],
        out_shape=out_shape,
        scratch_shapes=scratch,
        compiler_params=_cparams(2),
        name="attn_lat" if has_lat else "attn_ctx",
    )(*ins)


def _merge_kernel(h_ref, mod_ref, g_ref, oa_ref, ob_ref, oc_ref, gt_ref, wb_ref, wo_ref, o_ref):
    merged = None
    for i, br in enumerate((oa_ref, ob_ref, oc_ref)):
        p = _dot(br[0], wb_ref[i]) * gt_ref[0, :, i * D_MODEL:(i + 1) * D_MODEL].astype(F32)
        merged = p if merged is None else merged + p
    out = _dot(merged.astype(BF16), wo_ref[...])
    o_ref[0] = h_ref[0] + mod_ref[0, 2:3, :] * _rms(out, g_ref[1:2, :])


def _merge_call(h, mod3, g2, oa, ob, oc, gt, w_branch, w_out, tm):
    groups, rows, d = h.shape
    tok = lambda w: pl.BlockSpec((1, tm, w), lambda g, i: (g, i, 0))
    return pl.pallas_call(
        _merge_kernel,
        grid=(groups, rows // tm),
        in_specs=[tok(d), pl.BlockSpec((1, 3, d), lambda g, i: (g, 0, 0)), _const_spec(g2.shape),
                  tok(BRANCH_W), tok(BRANCH_W), tok(BRANCH_W), tok(N_BRANCH * d),
                  _const_spec(w_branch.shape), _const_spec(w_out.shape)],
        out_specs=tok(d),
        out_shape=jax.ShapeDtypeStruct(h.shape, F32),
        compiler_params=_cparams(2),
        name="merge",
    )(h, mod3, g2, oa, ob, oc, gt, w_branch, w_out)


def _rope_tables(n_tok):
    t = np.arange(n_tok)
    n_freq = ROPE_DIM // 4
    inv = ROPE_THETA ** (-np.arange(n_freq, dtype=np.float64) / n_freq)
    ang = np.concatenate([(t // GRID_W)[:, None] * inv, (t % GRID_W)[:, None] * inv], axis=-1)
    cos, sin = np.cos(ang), np.sin(ang)
    cos128 = np.tile(cos, (1, LANES // (ROPE_DIM // 2)))
    sin128 = np.tile(np.concatenate([-sin, sin], axis=-1), (1, LANES // ROPE_DIM))
    return jnp.asarray(cos128, F32), jnp.asarray(sin128, F32)


def _ones_block_diag():
    i = np.arange(LANES)
    return jnp.asarray((i[:, None] // HEAD_DIM) == (i[None, :] // HEAD_DIM), BF16)


def kernel(x, c, ctx, c_ctx, w_mod, b_mod, norm_g, w_ffn_in, w_ffn_out, w_in, b_gate, diff_lambda,
           diff_norm_g, ret_decay_logit, ret_norm_g, qk_norm_g, w_branch, w_out):
    batch, seq, d = x.shape
    ctx_len = ctx.shape[1]
    depth = w_mod.shape[0]
    assert d == D_MODEL and seq % 512 == 0 and (batch * ctx_len) % 512 == 0

    pad = (-(batch + 1)) % 8
    c_all = jnp.concatenate([c, c_ctx[None, :], jnp.zeros((pad, d), F32)], axis=0)
    mod_all = _mod_call(c_all, w_mod, b_mod).reshape(depth, batch + 1 + pad, N_SUB, 3, d)

    wi = w_ffn_in.astype(BF16)
    wo = w_ffn_out.astype(BF16)
    w_in_b = w_in.astype(BF16)
    wb = w_branch.astype(BF16)
    wout = w_out.astype(BF16)
    cos, sin = _rope_tables(seq)
    ones_bd = _ones_block_diag()
    qk_gain = jnp.tile(qk_norm_g, (1, 1, LANES // HEAD_DIM))

    h = x
    hc = ctx.reshape(1, batch * ctx_len, d)
    for l in range(depth):
        last = l == depth - 1
        lam_init = 0.8 - 0.6 * math.exp(-0.3 * l)
        mod_l = lambda s: mod_all[l, :batch, s]
        mod_c = lambda s: mod_all[l, batch:batch + 1, s]
        ng = norm_g[l]

        h = _ffn_call(h, mod_l(0), ng[0:2], wi[l, 0], wo[l, 0], 512)
        hc = _ffn_call(hc, mod_c(0), ng[0:2], wi[l, 0], wo[l, 0], 512)

        bg = b_gate[l][None, :]
        p_lat = _proj_call(h, mod_l(1), ng[2:4], w_in_b[l], bg, qk_gain[l], ones_bd, cos, sin, 512, True)
        p_ctx = _proj_call(hc, mod_c(1), ng[2:4], w_in_b[l], bg, qk_gain[l], ones_bd, cos, sin, 512, False)
        p_ctx = {k: v.reshape(batch, ctx_len, v.shape[-1]) for k, v in p_ctx.items()}

        ob, ob_c = _ret_call(p_lat, p_ctx, ret_decay_logit[l], ret_norm_g[l][None, :], not last)
        dgain = diff_norm_g[l][None, :]
        oa, oc = _attn_call(p_lat, p_lat, p_ctx, diff_lambda[l], dgain, lam_init, 256)
        h = _merge_call(h, mod_l(1), ng[2:4], oa, ob, oc, p_lat['gt'], wb[l], wout[l], 512)
        if not last:
            oa_c, oc_c = _attn_call(p_ctx, None, p_ctx, diff_lambda[l], dgain, lam_init, ctx_len)
            flat = lambda a: a.reshape(1, batch * ctx_len, a.shape[-1])
            hc = _merge_call(hc, mod_c(1), ng[2:4], flat(oa_c), flat(ob_c), flat(oc_c), flat(p_ctx['gt']),
                             wb[l], wout[l], 512)

        h = _ffn_call(h, mod_l(2), ng[4:6], wi[l, 1], wo[l, 1], 512)
        if not last:
            hc = _ffn_call(hc, mod_c(2), ng[4:6], wi[l, 1], wo[l, 1], 512)
    return h
```

```python
import functools
import math

import numpy as np
import jax
import jax.numpy as jnp
from jax import lax
from jax.experimental import pallas as pl
from jax.experimental.pallas import tpu as pltpu

F32 = jnp.float32
BF16 = jnp.bfloat16

D_MODEL = 1024
N_SUB = 3
D_FF = 2816
EPS = 1e-6
ROPE_THETA = 10000.0
ROPE_DIM = 64
GRID_W = 64
HEAD_DIM = 64
LANES = 128
DA_HEADS = 4
RET_HEADS = 4
RET_DV = 128
GQA_HEADS = 8
GQA_KV = 2
N_BRANCH = 3
BRANCH_W = 512
QK_SCALE = HEAD_DIM ** -0.5
SOFTMAX_Q_SCALE = QK_SCALE * math.log2(math.e)
MXU_TILE = 256
RET_CHUNK = MXU_TILE
ATTN_KEY_BLOCK = MXU_TILE
TOKEN_TILE = 2 * MXU_TILE
ATTN_Q_TILE = MXU_TILE
PROJ_ROW_SUB = MXU_TILE
ONES_ROWS = 16
DA_VROWS = LANES + ONES_ROWS
GQA_VROWS = HEAD_DIM + ONES_ROWS
VMEM_LIMIT = 56 * 1024 * 1024

_SIZES = (512, 512, 512, 256, 256, 512, 512, 512, 128, 128, N_BRANCH * D_MODEL)
_NAMES = ('a_q', 'a_k', 'a_v', 'b_q', 'b_k', 'b_v', 'b_g', 'c_q', 'c_k', 'c_v', 'gate')
_SPAN = {}
_lo = 0
for _n, _s in zip(_NAMES, _SIZES):
    _SPAN[_n] = (_lo, _lo + _s)
    _lo += _s
IN_COLS = _lo


def _cparams(n_axes):
    return pltpu.CompilerParams(dimension_semantics=("arbitrary",) * n_axes,
                                vmem_limit_bytes=VMEM_LIMIT)


def _const_spec(shape):
    nd = len(shape)
    return pl.BlockSpec(shape, lambda *_: (0,) * nd, pipeline_mode=pl.Buffered(1))


def _rms(x, g):
    return x * lax.rsqrt(jnp.mean(x * x, axis=-1, keepdims=True) + EPS) * g


def _sigmoid(x):
    return 1.0 / (1.0 + jnp.exp(-x))


def _dot(a, b):
    return jnp.dot(a, b, preferred_element_type=F32)


def _dot_nt(a, b):
    return lax.dot_general(a, b, (((1,), (1,)), ((), ())), preferred_element_type=F32)


def _dot_tn(a, b):
    return lax.dot_general(a, b, (((0,), (0,)), ((), ())), preferred_element_type=F32)


def _mod_kernel(c_ref, w_ref, b_ref, o_ref):
    c = c_ref[...]
    x = (c * _sigmoid(c)).astype(BF16)
    o_ref[0] = _dot(x, w_ref[0].astype(BF16)) + b_ref[0]


def _mod_call(c_all, w_mod, b_mod):
    depth, d, cols = w_mod.shape
    rows = c_all.shape[0]
    tn = 1536
    return pl.pallas_call(
        _mod_kernel,
        grid=(depth, cols // tn),
        in_specs=[pl.BlockSpec((rows, d), lambda l, j: (0, 0)),
                  pl.BlockSpec((1, d, tn), lambda l, j: (l, 0, j)),
                  pl.BlockSpec((1, 1, tn), lambda l, j: (l, 0, j))],
        out_specs=pl.BlockSpec((1, rows, tn), lambda l, j: (l, 0, j)),
        out_shape=jax.ShapeDtypeStruct((depth, rows, cols), F32),
        compiler_params=_cparams(2),
        name="mod",
    )(c_all, w_mod, b_mod.reshape(depth, 1, cols))


def _row_blocks(tm, sub):
    return [slice(r, r + sub) for r in range(0, tm, sub)]


def _ffn_kernel(x_ref, mod_ref, g_ref, wi_ref, wo_ref, o_ref):
    shift, scale, gate = mod_ref[0, 0:1, :], mod_ref[0, 1:2, :], mod_ref[0, 2:3, :]
    x = x_ref[0]
    y = (_rms(x, g_ref[0:1, :]) * (1.0 + scale) + shift).astype(BF16)
    acc = None
    for c in range(D_FF // MXU_TILE):
        lo = c * MXU_TILE
        a = _dot(y, wi_ref[:, lo:lo + MXU_TILE])
        u = _dot(y, wi_ref[:, D_FF + lo:D_FF + lo + MXU_TILE])
        hid = (a * _sigmoid(a) * u).astype(BF16)
        part = _dot(hid, wo_ref[lo:lo + MXU_TILE, :])
        acc = part if acc is None else acc + part
    o_ref[0] = x + 0.5 * gate * _rms(acc, g_ref[1:2, :])


def _layer_spec(stacked, lead):
    shape = stacked.shape[len(lead):]
    nd = len(shape)
    return pl.BlockSpec((None,) * len(lead) + shape, lambda *_: tuple(lead) + (0,) * nd,
                        pipeline_mode=pl.Buffered(1))


def _ffn_call(x, mod3, g2, wi, wo, lead, tm):
    groups, rows, d = x.shape
    return pl.pallas_call(
        _ffn_kernel,
        grid=(groups, rows // tm),
        in_specs=[pl.BlockSpec((1, tm, d), lambda g, i: (g, i, 0)),
                  pl.BlockSpec((1, 3, d), lambda g, i: (g, 0, 0)),
                  _const_spec(g2.shape), _layer_spec(wi, lead), _layer_spec(wo, lead)],
        out_specs=pl.BlockSpec((1, tm, d), lambda g, i: (g, i, 0)),
        out_shape=jax.ShapeDtypeStruct(x.shape, F32),
        compiler_params=_cparams(2),
        name="ffn",
    )(x, mod3, g2, wi, wo)


def _swap_half_heads(x):
    lane = lax.broadcasted_iota(jnp.int32, x.shape, 1)
    return jnp.where((lane & (HEAD_DIM - 1)) < HEAD_DIM // 2,
                     pltpu.roll(x, LANES - HEAD_DIM // 2, axis=1),
                     pltpu.roll(x, HEAD_DIM // 2, axis=1))


def _rope_cols(p, cos, sin):
    cols = []
    for j in range(p.shape[1] // LANES):
        x = p[:, j * LANES:(j + 1) * LANES]
        cols.append(x * cos + _swap_half_heads(x) * sin)
    return jnp.concatenate(cols, axis=1) if len(cols) > 1 else cols[0]


def _head_rms_cols(p, gain, ones_bd):
    cols = []
    for j in range(p.shape[1] // LANES):
        x = p[:, j * LANES:(j + 1) * LANES]
        xx = x * x
        hi = xx.astype(BF16)
        lo = (xx - hi.astype(F32)).astype(BF16)
        ms = (_dot(hi, ones_bd) + _dot(lo, ones_bd)) * (1.0 / HEAD_DIM)
        cols.append(x * lax.rsqrt(ms + EPS) * gain)
    return jnp.concatenate(cols, axis=1) if len(cols) > 1 else cols[0]


def _dup_heads(x):
    lane = lax.broadcasted_iota(jnp.int32, x.shape, 1)
    r = pltpu.roll(x, HEAD_DIM, axis=1)
    low = lane < HEAD_DIM
    return jnp.concatenate([jnp.where(low, x, r), jnp.where(low, r, x)], axis=1)


def _proj_kernel(x_ref, mod_ref, g_ref, w_ref, wvt_ref, bg_ref, qkg_ref, bd_ref, cos_ref, sin_ref,
                 qa_ref, ka_ref, qb_ref, kb_ref, vb_ref, gb_ref, qc_ref, kc_ref, gt_ref, vat_ref, vct_ref,
                 *, rope):
    shift, scale = mod_ref[0, 0:1, :], mod_ref[0, 1:2, :]
    bd = bd_ref[...]
    ones = jnp.ones((ONES_ROWS, PROJ_ROW_SUB), BF16)
    for rows in _row_blocks(x_ref.shape[1], PROJ_ROW_SUB):
        x = x_ref[0, rows, :]
        y = (_rms(x, g_ref[0:1, :]) * (1.0 + scale) + shift).astype(BF16)
        cos, sin = cos_ref[rows, :], sin_ref[rows, :]

        def proj(name):
            lo, hi = _SPAN[name]
            return _dot(y, w_ref[:, lo:hi])

        def pos(p):
            return _rope_cols(p, cos, sin) if rope else p

        qa_ref[0, rows, :] = (pos(proj('a_q')) * SOFTMAX_Q_SCALE).astype(BF16)
        ka_ref[0, rows, :] = pos(proj('a_k')).astype(BF16)
        qb_ref[0, rows, :] = pos(proj('b_q')) * QK_SCALE
        kb_ref[0, rows, :] = pos(proj('b_k'))
        vb_ref[0, rows, :] = proj('b_v').astype(BF16)
        g = proj('b_g')
        gb_ref[0, rows, :] = (g * _sigmoid(g)).astype(BF16)
        qc_ref[0, rows, :] = (pos(_head_rms_cols(proj('c_q'), qkg_ref[0:1, :], bd))
                              * SOFTMAX_Q_SCALE).astype(BF16)
        kc_ref[0, rows, :] = _dup_heads(pos(_head_rms_cols(proj('c_k'), qkg_ref[1:2, :], bd))).astype(BF16)
        glo = _SPAN['gate'][0]
        for i in range(N_BRANCH):
            lo = glo + i * D_MODEL
            pre = _dot(y, w_ref[:, lo:lo + D_MODEL]) + bg_ref[:, i * D_MODEL:(i + 1) * D_MODEL]
            gt_ref[0, rows, i * D_MODEL:(i + 1) * D_MODEL] = _sigmoid(pre).astype(BF16)
        vt = _dot_nt(wvt_ref[...], y).astype(BF16)
        for c in range(DA_HEADS):
            vat_ref[0, c * DA_VROWS:c * DA_VROWS + LANES, rows] = vt[c * LANES:(c + 1) * LANES]
            vat_ref[0, c * DA_VROWS + LANES:(c + 1) * DA_VROWS, rows] = ones
        for g_ in range(GQA_KV):
            src = DA_HEADS * LANES + g_ * HEAD_DIM
            vct_ref[0, g_ * GQA_VROWS:g_ * GQA_VROWS + HEAD_DIM, rows] = vt[src:src + HEAD_DIM]
            vct_ref[0, g_ * GQA_VROWS + HEAD_DIM:(g_ + 1) * GQA_VROWS, rows] = ones


_PROJ_OUT = (('qa', 512, BF16), ('ka', 512, BF16), ('qb', 256, F32), ('kb', 256, F32),
             ('vb', 512, BF16), ('gb', 512, BF16), ('qc', 512, BF16), ('kc', 256, BF16),
             ('gt', N_BRANCH * D_MODEL, BF16))
_PROJ_OUT_T = (('vat', DA_HEADS * DA_VROWS), ('vct', GQA_KV * GQA_VROWS))


def _proj_call(x, mod3, g2, w_in, wvt, b_gate, layer, qk_gain, ones_bd, cos, sin, tm, rope):
    groups, rows, d = x.shape
    tok = lambda w: pl.BlockSpec((1, tm, w), lambda g, i: (g, i, 0))
    tok_t = lambda r: pl.BlockSpec((1, r, tm), lambda g, i: (g, 0, i))
    pos_map = (lambda g, i: (i, 0)) if rope else (lambda g, i: (0, 0))
    outs = pl.pallas_call(
        functools.partial(_proj_kernel, rope=rope),
        grid=(groups, rows // tm),
        in_specs=[tok(d),
                  pl.BlockSpec((1, 3, d), lambda g, i: (g, 0, 0)),
                  _const_spec(g2.shape), _layer_spec(w_in, (layer,)), _layer_spec(wvt, (layer,)),
                  _layer_spec(b_gate, (layer,)),
                  _const_spec(qk_gain.shape), _const_spec(ones_bd.shape),
                  pl.BlockSpec((tm, LANES), pos_map), pl.BlockSpec((tm, LANES), pos_map)],
        out_specs=[tok(w) for _, w, _ in _PROJ_OUT] + [tok_t(r) for _, r in _PROJ_OUT_T],
        out_shape=([jax.ShapeDtypeStruct((groups, rows, w), dt) for _, w, dt in _PROJ_OUT]
                   + [jax.ShapeDtypeStruct((groups, r, rows), BF16) for _, r in _PROJ_OUT_T]),
        compiler_params=_cparams(2),
        name="proj_rope" if rope else "proj_ctx",
    )(x, mod3, g2, w_in, wvt, b_gate, qk_gain, ones_bd, cos, sin)
    return dict(zip([n for n, _, _ in _PROJ_OUT] + [n for n, _ in _PROJ_OUT_T], outs))


def _log_sigmoid(x):
    return jnp.minimum(x, 0.0) - jnp.log(1.0 + jnp.exp(-jnp.abs(x)))


def _split_heads(col):
    lane = lax.broadcasted_iota(jnp.int32, col.shape, 1)
    zero = jnp.zeros_like(col)
    return jnp.concatenate([jnp.where(lane < HEAD_DIM, col, zero),
                            jnp.where(lane >= HEAD_DIM, col, zero)], axis=0)


def _ret_kernel(*refs, n_chunks, with_ctx_out):
    if with_ctx_out:
        (q_ref, k_ref, v_ref, g_ref, kc_ref, vc_ref, qc_ref, gc_ref, logit_ref, gain_ref,
         o_ref, oc_ref) = refs
    else:
        q_ref, k_ref, v_ref, g_ref, kc_ref, vc_ref, logit_ref, gain_ref, o_ref = refs
    C = RET_CHUNK
    di = lax.broadcasted_iota(jnp.int32, (C, C), 0) - lax.broadcasted_iota(jnp.int32, (C, C), 1)
    dpos = jnp.maximum(di, 0).astype(F32)
    dneg = jnp.maximum(-di, 0).astype(F32)
    pos = lax.broadcasted_iota(jnp.int32, (C, 1), 0).astype(F32)
    gain = gain_ref[...]

    tabs = []
    for h in range(RET_HEADS):
        lf = _log_sigmoid(logit_ref[0:1, h:h + 1])
        lb = _log_sigmoid(logit_ref[1:2, h:h + 1])
        tabs.append(dict(
            mask=jnp.where(di >= 0, jnp.exp(lf * dpos), 0.0) + jnp.where(di <= 0, jnp.exp(lb * dneg), 0.0),
            q_f=jnp.exp(lf * (pos + 1.0)), q_b=jnp.exp(lb * (C - pos)),
            k_f=jnp.exp(lf * (C - 1.0 - pos)), k_b=jnp.exp(lb * pos),
            s_f=jnp.exp(lf * C), s_b=jnp.exp(lb * C)))

    def chunk_states(kcol, vfull, h):
        v = vfull[:, h * RET_DV:(h + 1) * RET_DV]
        t = tabs[h]
        return (_dot_tn((kcol * t['k_f']).astype(BF16), v), _dot_tn((kcol * t['k_b']).astype(BF16), v))

    def finish(o, gate):
        return (_rms(o, gain) * gate.astype(F32)).astype(BF16)

    kc_all, vc_all = kc_ref[0], vc_ref[0]
    init = [chunk_states(kc_all[:, (h // 2) * LANES:(h // 2 + 1) * LANES], vc_all, h) for h in range(RET_HEADS)]

    sums = []
    for c in range(n_chunks):
        kc_ = k_ref[0, c * C:(c + 1) * C, :]
        vc_ = v_ref[0, c * C:(c + 1) * C, :]
        sums.append([chunk_states(kc_[:, (h // 2) * LANES:(h // 2 + 1) * LANES], vc_, h) for h in range(RET_HEADS)])
    fwd = [[None] * RET_HEADS for _ in range(n_chunks)]
    bwd = [[None] * RET_HEADS for _ in range(n_chunks)]
    for h in range(RET_HEADS):
        s = init[h][0]
        for c in range(n_chunks):
            fwd[c][h] = s
            s = s * tabs[h]['s_f'] + sums[c][h][0]
        s = init[h][1]
        for c in reversed(range(n_chunks)):
            bwd[c][h] = s
            s = s * tabs[h]['s_b'] + sums[c][h][1]

    def chunk_out(q, k, v, states):
        outs = []
        for j in range(RET_HEADS // 2):
            qs = _split_heads(q[:, j * LANES:(j + 1) * LANES])
            s = _dot_nt(qs.astype(BF16), k[:, j * LANES:(j + 1) * LANES].astype(BF16))
            for hh in range(2):
                h = 2 * j + hh
                t = tabs[h]
                att = (s[hh * C:(hh + 1) * C] * t['mask']).astype(BF16)
                o = _dot(att, v[:, h * RET_DV:(h + 1) * RET_DV])
                if states is not None:
                    qm = qs[hh * C:(hh + 1) * C]
                    sf, sb = states[h]
                    o = o + _dot((qm * t['q_f']).astype(BF16), sf.astype(BF16))
                    o = o + _dot((qm * t['q_b']).astype(BF16), sb.astype(BF16))
                outs.append(o)
        return outs

    for c in range(n_chunks):
        rows = slice(c * C, (c + 1) * C)
        outs = chunk_out(q_ref[0, rows, :], k_ref[0, rows, :], v_ref[0, rows, :],
                         [(fwd[c][h], bwd[c][h]) for h in range(RET_HEADS)])
        for h in range(RET_HEADS):
            cols = slice(h * RET_DV, (h + 1) * RET_DV)
            o_ref[0, rows, cols] = finish(outs[h], g_ref[0, rows, cols])

    if with_ctx_out:
        outs = chunk_out(qc_ref[0], kc_all, vc_all, None)
        for h in range(RET_HEADS):
            cols = slice(h * RET_DV, (h + 1) * RET_DV)
            oc_ref[0, :, cols] = finish(outs[h], gc_ref[0, :, cols])


def _ret_call(pl_, pc_, logit, gain, with_ctx_out):
    b, n, _ = pl_['qb'].shape
    lc = pc_['kb'].shape[1]
    assert lc == RET_CHUNK and n % RET_CHUNK == 0
    full = lambda a: pl.BlockSpec((1,) + a.shape[1:], lambda i: (i, 0, 0))
    ins = [pl_['qb'], pl_['kb'], pl_['vb'], pl_['gb'], pc_['kb'], pc_['vb']]
    if with_ctx_out:
        ins += [pc_['qb'], pc_['gb']]
    in_specs = [full(a) for a in ins] + [_const_spec(logit.shape), _const_spec(gain.shape)]
    out_shape = [jax.ShapeDtypeStruct((b, n, RET_HEADS * RET_DV), BF16)]
    if with_ctx_out:
        out_shape.append(jax.ShapeDtypeStruct((b, lc, RET_HEADS * RET_DV), BF16))
    outs = pl.pallas_call(
        functools.partial(_ret_kernel, n_chunks=n // RET_CHUNK, with_ctx_out=with_ctx_out),
        grid=(b,),
        in_specs=in_specs,
        out_specs=[pl.BlockSpec((1,) + s.shape[1:], lambda i: (i, 0, 0)) for s in out_shape],
        out_shape=out_shape,
        compiler_params=_cparams(1),
        name="retention_ctx" if with_ctx_out else "retention",
    )(*ins, logit, gain)
    return outs if with_ctx_out else (outs[0], None)


def _fold_rows(x, op):
    r, n = x.shape
    return op(x.reshape(r // 8, 8, n), axis=0)


def _attn_kernel(*refs, has_lat, lam_init):
    if has_lat:
        (qa_ref, qc_ref, ka_ref, kc_ref, vat_ref, vct_ref, kac_ref, kcc_ref, vact_ref, vcct_ref,
         dl_ref, dg_ref, oa_ref, oc_ref, s_ref) = refs
    else:
        qa_ref, qc_ref, kac_ref, kcc_ref, vact_ref, vcct_ref, dl_ref, dg_ref, oa_ref, oc_ref, s_ref = refs

    dl = dl_ref[...]
    lam = (jnp.exp(jnp.sum(dl[0:1] * dl[1:2], axis=-1, keepdims=True))
           - jnp.exp(jnp.sum(dl[2:3] * dl[3:4], axis=-1, keepdims=True)) + lam_init)
    dgain = dg_ref[...] * (1.0 - lam_init)
    tq = qa_ref.shape[1]
    n_cols = DA_HEADS + GQA_HEADS // 2
    t_lat = ka_ref.shape[1] if has_lat else 0
    t_ctx = kac_ref.shape[1]
    kb = ATTN_KEY_BLOCK
    assert t_lat % kb == 0 and t_ctx % kb == 0
    blocks = [(False, r) for r in range(0, t_lat, kb)] + [(True, r) for r in range(0, t_ctx, kb)]

    def column(c):
        if c < DA_HEADS:
            cols = slice(c * LANES, (c + 1) * LANES)
            return (qa_ref[0, :, cols], (ka_ref if has_lat else None, kac_ref), cols,
                    (vat_ref if has_lat else None, vact_ref), slice(c * DA_VROWS, (c + 1) * DA_VROWS), LANES)
        j = c - DA_HEADS
        g = j // 2
        return (qc_ref[0, :, slice(j * LANES, (j + 1) * LANES)], (kc_ref if has_lat else None, kcc_ref),
                slice(g * LANES, (g + 1) * LANES),
                (vct_ref if has_lat else None, vcct_ref), slice(g * GQA_VROWS, (g + 1) * GQA_VROWS), HEAD_DIM)

    runtime_zero = jnp.minimum(pl.program_id(1), 0)

    def score_block(c, qs, i):
        _, krefs, kcols, _, _, _ = column(c)
        is_ctx, r = blocks[i]
        s = _dot(krefs[is_ctx][0, r:r + kb, kcols], qs)
        s_ref[c % 2 + runtime_zero, i * kb:(i + 1) * kb, :] = s
        return _fold_rows(s, jnp.max)

    def value_block(c, m, i):
        _, _, _, vrefs, vrows, _ = column(c)
        is_ctx, r = blocks[i]
        e = jnp.exp2(s_ref[c % 2 + runtime_zero, i * kb:(i + 1) * kb, :] - m)
        return _dot(vrefs[is_ctx][0, vrows, r:r + kb], e.astype(BF16))

    def finish(c, o_ext):
        dv = column(c)[5]
        o = o_ext[:dv] * (1.0 / o_ext[dv:dv + 1])
        o1, o2 = o[:, :tq], o[:, tq:]
        if c < DA_HEADS:
            d = o1 - lam * o2
            d = d * lax.rsqrt(jnp.mean(d * d, axis=0, keepdims=True) + EPS)
            oa_ref[0, :, c * LANES:(c + 1) * LANES] = (d.T * dgain).astype(BF16)
        else:
            j = c - DA_HEADS
            oc_ref[0, :, j * LANES:(j + 1) * LANES] = jnp.concatenate([o1, o2], axis=0).T.astype(BF16)

    def tree(op, xs):
        return functools.reduce(op, xs)

    def split_heads_t(qcol):
        qt = qcol.astype(F32).T
        row = lax.broadcasted_iota(jnp.int32, qt.shape, 0)
        return jnp.concatenate([jnp.where(row < HEAD_DIM, qt, 0.0), jnp.where(row >= HEAD_DIM, qt, 0.0)],
                               axis=1).astype(BF16)

    qs = split_heads_t(column(0)[0])
    m = jnp.max(tree(jnp.maximum, [score_block(0, qs, i) for i in range(len(blocks))]), axis=0, keepdims=True)
    for c in range(n_cols):
        nxt = c + 1 < n_cols
        if nxt:
            qs_next = split_heads_t(column(c + 1)[0])
        maxes, outs = [], []
        for i in range(len(blocks)):
            if nxt:
                maxes.append(score_block(c + 1, qs_next, i))
            outs.append(value_block(c, m, i))
        finish(c, tree(lambda a, b: a + b, outs))
        if nxt:
            m = jnp.max(tree(jnp.maximum, maxes), axis=0, keepdims=True)


def _attn_call(pl_, pc_, batch, ctx_len, diff_lambda, diff_gain, lam_init, tq):
    has_lat = pl_ is not None
    ctx_tok = lambda w: pl.BlockSpec((1, ctx_len, w), lambda i, j: (0, i, 0))
    ctx_t = lambda r: pl.BlockSpec((1, r, ctx_len), lambda i, j: (0, 0, i))
    if has_lat:
        n = pl_['qa'].shape[1]
        qspec = lambda w: pl.BlockSpec((1, tq, w), lambda i, j: (i, j, 0))
        whole = lambda a: pl.BlockSpec((1,) + a.shape[1:], lambda i, j: (i, 0, 0))
        kvl = [pl_['ka'], pl_['kc'], pl_['vat'], pl_['vct']]
        ins = [pl_['qa'], pl_['qc']] + kvl
        in_specs = [qspec(512), qspec(512)] + [whole(a) for a in kvl]
        grid = (batch, n // tq)
        out_rows, t_total = (batch, n), n + ctx_len
    else:
        assert tq == ctx_len
        qspec = ctx_tok
        ins = [pc_['qa'], pc_['qc']]
        in_specs = [qspec(512), qspec(512)]
        grid = (batch, 1)
        out_rows, t_total = (1, batch * ctx_len), ctx_len
    ins += [pc_['ka'], pc_['kc'], pc_['vat'], pc_['vct'], diff_lambda, diff_gain]
    in_specs += [ctx_tok(512), ctx_tok(256), ctx_t(DA_HEADS * DA_VROWS), ctx_t(GQA_KV * GQA_VROWS),
                 _const_spec(diff_lambda.shape), _const_spec(diff_gain.shape)]
    return pl.pallas_call(
        functools.partial(_attn_kernel, has_lat=has_lat, lam_init=lam_init),
        grid=grid,
        in_specs=in_specs,
        out_specs=[qspec(BRANCH_W), qspec(BRANCH_W)],
        out_shape=[jax.ShapeDtypeStruct(out_rows + (BRANCH_W,), BF16)] * 2,
        scratch_shapes=[pltpu.VMEM((2, t_total, 2 * tq), F32)],
        compiler_params=_cparams(2),
        name="attn_lat" if has_lat else "attn_ctx",
    )(*ins)


def _merge_kernel(h_ref, mod_ref, g_ref, oa_ref, ob_ref, oc_ref, gt_ref, wb_ref, wo_ref, o_ref):
    merged = None
    for i, br in enumerate((oa_ref, ob_ref, oc_ref)):
        p = _dot(br[0], wb_ref[i]) * gt_ref[0, :, i * D_MODEL:(i + 1) * D_MODEL].astype(F32)
        merged = p if merged is None else merged + p
    out = _dot(merged.astype(BF16), wo_ref[...])
    o_ref[0] = h_ref[0] + mod_ref[0, 2:3, :] * _rms(out, g_ref[1:2, :])


def _merge_call(h, mod3, g2, oa, ob, oc, gt, w_branch, w_out, layer, tm):
    groups, rows, d = h.shape
    tok = lambda w: pl.BlockSpec((1, tm, w), lambda g, i: (g, i, 0))
    return pl.pallas_call(
        _merge_kernel,
        grid=(groups, rows // tm),
        in_specs=[tok(d), pl.BlockSpec((1, 3, d), lambda g, i: (g, 0, 0)), _const_spec(g2.shape),
                  tok(BRANCH_W), tok(BRANCH_W), tok(BRANCH_W), tok(N_BRANCH * d),
                  _layer_spec(w_branch, (layer,)), _layer_spec(w_out, (layer,))],
        out_specs=tok(d),
        out_shape=jax.ShapeDtypeStruct(h.shape, F32),
        compiler_params=_cparams(2),
        name="merge",
    )(h, mod3, g2, oa, ob, oc, gt, w_branch, w_out)


def _rope_tables(n_tok):
    t = np.arange(n_tok)
    n_freq = ROPE_DIM // 4
    inv = ROPE_THETA ** (-np.arange(n_freq, dtype=np.float64) / n_freq)
    ang = np.concatenate([(t // GRID_W)[:, None] * inv, (t % GRID_W)[:, None] * inv], axis=-1)
    cos, sin = np.cos(ang), np.sin(ang)
    cos128 = np.tile(cos, (1, LANES // (ROPE_DIM // 2)))
    sin128 = np.tile(np.concatenate([-sin, sin], axis=-1), (1, LANES // ROPE_DIM))
    return jnp.asarray(cos128, F32), jnp.asarray(sin128, F32)


def _ones_block_diag():
    i = np.arange(LANES)
    return jnp.asarray((i[:, None] // HEAD_DIM) == (i[None, :] // HEAD_DIM), BF16)


def kernel(x, c, ctx, c_ctx, w_mod, b_mod, norm_g, w_ffn_in, w_ffn_out, w_in, b_gate, diff_lambda,
           diff_norm_g, ret_decay_logit, ret_norm_g, qk_norm_g, w_branch, w_out):
    batch, seq, d = x.shape
    ctx_len = ctx.shape[1]
    depth = w_mod.shape[0]
    assert d == D_MODEL and seq % TOKEN_TILE == 0 and (batch * ctx_len) % TOKEN_TILE == 0

    pad = (-(batch + 1)) % 8
    c_all = jnp.concatenate([c, c_ctx[None, :], jnp.zeros((pad, d), F32)], axis=0)
    mod_all = _mod_call(c_all, w_mod, b_mod).reshape(depth, batch + 1 + pad, N_SUB, 3, d)

    wi = w_ffn_in.astype(BF16)
    wo = w_ffn_out.astype(BF16)
    w_in_b = w_in.astype(BF16)
    wb = w_branch.astype(BF16)
    wout = w_out.astype(BF16)
    (av_lo, av_hi), (cv_lo, cv_hi) = _SPAN['a_v'], _SPAN['c_v']
    wvt = jnp.concatenate([w_in[:, :, av_lo:av_hi], w_in[:, :, cv_lo:cv_hi]], axis=2
                          ).transpose(0, 2, 1).astype(BF16)
    bg = b_gate[:, None, :]
    cos, sin = _rope_tables(seq)
    ones_bd = _ones_block_diag()
    qk_gain = jnp.tile(qk_norm_g, (1, 1, LANES // HEAD_DIM))

    h = x
    hc = ctx.reshape(1, batch * ctx_len, d)
    for l in range(depth):
        last = l == depth - 1
        lam_init = 0.8 - 0.6 * math.exp(-0.3 * l)
        mod_l = lambda s: mod_all[l, :batch, s]
        mod_c = lambda s: mod_all[l, batch:batch + 1, s]
        ng = norm_g[l]

        h = _ffn_call(h, mod_l(0), ng[0:2], wi, wo, (l, 0), TOKEN_TILE)
        hc = _ffn_call(hc, mod_c(0), ng[0:2], wi, wo, (l, 0), TOKEN_TILE)

        p_lat = _proj_call(h, mod_l(1), ng[2:4], w_in_b, wvt, bg, l, qk_gain[l], ones_bd, cos, sin,
                           TOKEN_TILE, True)
        p_ctx = _proj_call(hc, mod_c(1), ng[2:4], w_in_b, wvt, bg, l, qk_gain[l], ones_bd, cos, sin,
                           TOKEN_TILE, False)
        per_batch = lambda a: a.reshape(batch, ctx_len, a.shape[-1])
        p_ctx_b = {k: per_batch(p_ctx[k]) for k in ('qb', 'kb', 'vb', 'gb')}

        ob, ob_c = _ret_call(p_lat, p_ctx_b, ret_decay_logit[l], ret_norm_g[l][None, :], not last)
        dgain = diff_norm_g[l][None, :]
        oa, oc = _attn_call(p_lat, p_ctx, batch, ctx_len, diff_lambda[l], dgain, lam_init, ATTN_Q_TILE)
        h = _merge_call(h, mod_l(1), ng[2:4], oa, ob, oc, p_lat['gt'], wb, wout, l, TOKEN_TILE)
        if not last:
            oa_c, oc_c = _attn_call(None, p_ctx, batch, ctx_len, diff_lambda[l], dgain, lam_init, ctx_len)
            hc = _merge_call(hc, mod_c(1), ng[2:4], oa_c, ob_c.reshape(1, batch * ctx_len, BRANCH_W), oc_c,
                             p_ctx['gt'], wb, wout, l, TOKEN_TILE)

        h = _ffn_call(h, mod_l(2), ng[4:6], wi, wo, (l, 1), TOKEN_TILE)
        if not last:
            hc = _ffn_call(hc, mod_c(2), ng[4:6], wi, wo, (l, 1), TOKEN_TILE)
    return h
```

```python
import functools
import math

import numpy as np
import jax
import jax.numpy as jnp
from jax import lax
from jax.experimental import pallas as pl
from jax.experimental.pallas import tpu as pltpu

F32 = jnp.float32
BF16 = jnp.bfloat16

D_MODEL = 1024
N_SUB = 3
D_FF = 2816
EPS = 1e-6
ROPE_THETA = 10000.0
ROPE_DIM = 64
GRID_W = 64
HEAD_DIM = 64
LANES = 128
DA_HEADS = 4
RET_HEADS = 4
RET_DV = 128
GQA_HEADS = 8
GQA_KV = 2
N_BRANCH = 3
BRANCH_W = 512
QK_SCALE = HEAD_DIM ** -0.5
SOFTMAX_Q_SCALE = QK_SCALE * math.log2(math.e)
MXU_TILE = 256
RET_CHUNK = MXU_TILE
ATTN_KEY_BLOCK = MXU_TILE
TOKEN_TILE = 2 * MXU_TILE
ATTN_Q_SUB = MXU_TILE
ATTN_Q_TILE = 2 * ATTN_Q_SUB
PROJ_ROW_SUB = MXU_TILE
ONES_ROWS = 16
DA_VROWS = LANES + ONES_ROWS
GQA_VROWS = HEAD_DIM + ONES_ROWS
VMEM_LIMIT = 56 * 1024 * 1024

_SIZES = (512, 512, 512, 256, 256, 512, 512, 512, 128, 128, N_BRANCH * D_MODEL)
_NAMES = ('a_q', 'a_k', 'a_v', 'b_q', 'b_k', 'b_v', 'b_g', 'c_q', 'c_k', 'c_v', 'gate')
_SPAN = {}
_lo = 0
for _n, _s in zip(_NAMES, _SIZES):
    _SPAN[_n] = (_lo, _lo + _s)
    _lo += _s
IN_COLS = _lo


def _cparams(n_axes):
    return pltpu.CompilerParams(dimension_semantics=("arbitrary",) * n_axes,
                                vmem_limit_bytes=VMEM_LIMIT)


def _const_spec(shape):
    nd = len(shape)
    return pl.BlockSpec(shape, lambda *_: (0,) * nd, pipeline_mode=pl.Buffered(1))


def _rms(x, g):
    return x * lax.rsqrt(jnp.mean(x * x, axis=-1, keepdims=True) + EPS) * g


def _sigmoid(x):
    return 1.0 / (1.0 + jnp.exp(-x))


def _dot(a, b):
    return jnp.dot(a, b, preferred_element_type=F32)


def _dot_nt(a, b):
    return lax.dot_general(a, b, (((1,), (1,)), ((), ())), preferred_element_type=F32)


def _dot_tn(a, b):
    return lax.dot_general(a, b, (((0,), (0,)), ((), ())), preferred_element_type=F32)


def _mod_kernel(c_ref, w_ref, b_ref, o_ref):
    c = c_ref[...]
    x = (c * _sigmoid(c)).astype(BF16)
    o_ref[0] = _dot(x, w_ref[0].astype(BF16)) + b_ref[0]


def _mod_call(c_all, w_mod, b_mod):
    depth, d, cols = w_mod.shape
    rows = c_all.shape[0]
    tn = 1536
    return pl.pallas_call(
        _mod_kernel,
        grid=(depth, cols // tn),
        in_specs=[pl.BlockSpec((rows, d), lambda l, j: (0, 0)),
                  pl.BlockSpec((1, d, tn), lambda l, j: (l, 0, j)),
                  pl.BlockSpec((1, 1, tn), lambda l, j: (l, 0, j))],
        out_specs=pl.BlockSpec((1, rows, tn), lambda l, j: (l, 0, j)),
        out_shape=jax.ShapeDtypeStruct((depth, rows, cols), F32),
        compiler_params=_cparams(2),
        name="mod",
    )(c_all, w_mod, b_mod.reshape(depth, 1, cols))


def _wvt_kernel(a_ref, c_ref, o_ref):
    n_a = a_ref.shape[1]
    o_ref[0:n_a, :] = a_ref[...].T.astype(BF16)
    o_ref[n_a:, :] = c_ref[...].T.astype(BF16)


def _wvt_call(w_in):
    depth, d, _ = w_in.shape
    (a_lo, a_hi), (c_lo, c_hi) = _SPAN['a_v'], _SPAN['c_v']
    n_a, n_c = a_hi - a_lo, c_hi - c_lo
    assert a_lo % n_a == 0 and c_lo % n_c == 0
    return pl.pallas_call(
        _wvt_kernel,
        grid=(depth,),
        in_specs=[pl.BlockSpec((None, d, n_a), lambda l: (l, 0, a_lo // n_a)),
                  pl.BlockSpec((None, d, n_c), lambda l: (l, 0, c_lo // n_c))],
        out_specs=pl.BlockSpec((None, n_a + n_c, d), lambda l: (l, 0, 0)),
        out_shape=jax.ShapeDtypeStruct((depth, n_a + n_c, d), BF16),
        compiler_params=_cparams(1),
        name="wvt",
    )(w_in, w_in)


def _row_blocks(tm, sub):
    return [slice(r, r + sub) for r in range(0, tm, sub)]


def _ffn_kernel(x_ref, mod_ref, g_ref, wi_ref, wo_ref, o_ref):
    shift, scale, gate = mod_ref[0, 0:1, :], mod_ref[0, 1:2, :], mod_ref[0, 2:3, :]
    x = x_ref[0]
    y = (_rms(x, g_ref[0:1, :]) * (1.0 + scale) + shift).astype(BF16)
    acc = None
    for c in range(D_FF // MXU_TILE):
        lo = c * MXU_TILE
        a = _dot(y, wi_ref[:, lo:lo + MXU_TILE])
        u = _dot(y, wi_ref[:, D_FF + lo:D_FF + lo + MXU_TILE])
        hid = (a * _sigmoid(a) * u).astype(BF16)
        part = _dot(hid, wo_ref[lo:lo + MXU_TILE, :])
        acc = part if acc is None else acc + part
    o_ref[0] = x + 0.5 * gate * _rms(acc, g_ref[1:2, :])


def _layer_spec(stacked, lead):
    shape = stacked.shape[len(lead):]
    nd = len(shape)
    return pl.BlockSpec((None,) * len(lead) + shape, lambda *_: tuple(lead) + (0,) * nd,
                        pipeline_mode=pl.Buffered(1))


def _ffn_call(x, mod3, g2, wi, wo, lead, tm):
    groups, rows, d = x.shape
    return pl.pallas_call(
        _ffn_kernel,
        grid=(groups, rows // tm),
        in_specs=[pl.BlockSpec((1, tm, d), lambda g, i: (g, i, 0)),
                  pl.BlockSpec((1, 3, d), lambda g, i: (g, 0, 0)),
                  _const_spec(g2.shape), _layer_spec(wi, lead), _layer_spec(wo, lead)],
        out_specs=pl.BlockSpec((1, tm, d), lambda g, i: (g, i, 0)),
        out_shape=jax.ShapeDtypeStruct(x.shape, F32),
        compiler_params=_cparams(2),
        name="ffn",
    )(x, mod3, g2, wi, wo)


def _swap_half_heads(x):
    lane = lax.broadcasted_iota(jnp.int32, x.shape, 1)
    return jnp.where((lane & (HEAD_DIM - 1)) < HEAD_DIM // 2,
                     pltpu.roll(x, LANES - HEAD_DIM // 2, axis=1),
                     pltpu.roll(x, HEAD_DIM // 2, axis=1))


def _rope_cols(p, cos, sin):
    cols = []
    for j in range(p.shape[1] // LANES):
        x = p[:, j * LANES:(j + 1) * LANES]
        cols.append(x * cos + _swap_half_heads(x) * sin)
    return jnp.concatenate(cols, axis=1) if len(cols) > 1 else cols[0]


def _head_rms_cols(p, gain, ones_bd):
    cols = []
    for j in range(p.shape[1] // LANES):
        x = p[:, j * LANES:(j + 1) * LANES]
        xx = x * x
        hi = xx.astype(BF16)
        lo = (xx - hi.astype(F32)).astype(BF16)
        ms = (_dot(hi, ones_bd) + _dot(lo, ones_bd)) * (1.0 / HEAD_DIM)
        cols.append(x * lax.rsqrt(ms + EPS) * gain)
    return jnp.concatenate(cols, axis=1) if len(cols) > 1 else cols[0]


def _dup_heads(x):
    lane = lax.broadcasted_iota(jnp.int32, x.shape, 1)
    r = pltpu.roll(x, HEAD_DIM, axis=1)
    low = lane < HEAD_DIM
    return jnp.concatenate([jnp.where(low, x, r), jnp.where(low, r, x)], axis=1)


def _proj_kernel(x_ref, mod_ref, g_ref, w_ref, wvt_ref, bg_ref, qkg_ref, bd_ref, cos_ref, sin_ref,
                 qa_ref, ka_ref, qb_ref, kb_ref, vb_ref, gb_ref, qc_ref, kc_ref, gt_ref, vat_ref, vct_ref,
                 *, rope):
    shift, scale = mod_ref[0, 0:1, :], mod_ref[0, 1:2, :]
    bd = bd_ref[...]
    ones = jnp.ones((ONES_ROWS, PROJ_ROW_SUB), BF16)
    for rows in _row_blocks(x_ref.shape[1], PROJ_ROW_SUB):
        x = x_ref[0, rows, :]
        y = (_rms(x, g_ref[0:1, :]) * (1.0 + scale) + shift).astype(BF16)
        cos, sin = cos_ref[rows, :], sin_ref[rows, :]

        def proj(name):
            lo, hi = _SPAN[name]
            return _dot(y, w_ref[:, lo:hi])

        def pos(p):
            return _rope_cols(p, cos, sin) if rope else p

        cq, ck = proj('c_q'), proj('c_k')
        qa_ref[0, rows, :] = (pos(proj('a_q')) * SOFTMAX_Q_SCALE).astype(BF16)
        ka_ref[0, rows, :] = pos(proj('a_k')).astype(BF16)
        qc_ref[0, rows, :] = (pos(_head_rms_cols(cq, qkg_ref[0:1, :], bd)) * SOFTMAX_Q_SCALE).astype(BF16)
        kc_ref[0, rows, :] = _dup_heads(pos(_head_rms_cols(ck, qkg_ref[1:2, :], bd))).astype(BF16)
        qb_ref[0, rows, :] = pos(proj('b_q')) * QK_SCALE
        kb_ref[0, rows, :] = pos(proj('b_k'))
        vb_ref[0, rows, :] = proj('b_v').astype(BF16)
        g = proj('b_g')
        gb_ref[0, rows, :] = (g * _sigmoid(g)).astype(BF16)
        glo = _SPAN['gate'][0]
        for i in range(N_BRANCH):
            lo = glo + i * D_MODEL
            pre = _dot(y, w_ref[:, lo:lo + D_MODEL]) + bg_ref[:, i * D_MODEL:(i + 1) * D_MODEL]
            gt_ref[0, rows, i * D_MODEL:(i + 1) * D_MODEL] = _sigmoid(pre).astype(BF16)
        vt = _dot_nt(wvt_ref[...], y).astype(BF16)
        for c in range(DA_HEADS):
            vat_ref[0, c * DA_VROWS:c * DA_VROWS + LANES, rows] = vt[c * LANES:(c + 1) * LANES]
            vat_ref[0, c * DA_VROWS + LANES:(c + 1) * DA_VROWS, rows] = ones
        for g_ in range(GQA_KV):
            src = DA_HEADS * LANES + g_ * HEAD_DIM
            vct_ref[0, g_ * GQA_VROWS:g_ * GQA_VROWS + HEAD_DIM, rows] = vt[src:src + HEAD_DIM]
            vct_ref[0, g_ * GQA_VROWS + HEAD_DIM:(g_ + 1) * GQA_VROWS, rows] = ones


_PROJ_OUT = (('qa', 512, BF16), ('ka', 512, BF16), ('qb', 256, F32), ('kb', 256, F32),
             ('vb', 512, BF16), ('gb', 512, BF16), ('qc', 512, BF16), ('kc', 256, BF16),
             ('gt', N_BRANCH * D_MODEL, BF16))
_PROJ_OUT_T = (('vat', DA_HEADS * DA_VROWS), ('vct', GQA_KV * GQA_VROWS))


def _proj_call(x, mod3, g2, w_in, wvt, b_gate, layer, qk_gain, ones_bd, cos, sin, tm, rope):
    groups, rows, d = x.shape
    tok = lambda w: pl.BlockSpec((1, tm, w), lambda g, i: (g, i, 0))
    tok_t = lambda r: pl.BlockSpec((1, r, tm), lambda g, i: (g, 0, i))
    pos_map = (lambda g, i: (i, 0)) if rope else (lambda g, i: (0, 0))
    outs = pl.pallas_call(
        functools.partial(_proj_kernel, rope=rope),
        grid=(groups, rows // tm),
        in_specs=[tok(d),
                  pl.BlockSpec((1, 3, d), lambda g, i: (g, 0, 0)),
                  _const_spec(g2.shape), _layer_spec(w_in, (layer,)), _layer_spec(wvt, (layer,)),
                  _layer_spec(b_gate, (layer,)),
                  _const_spec(qk_gain.shape), _const_spec(ones_bd.shape),
                  pl.BlockSpec((tm, LANES), pos_map), pl.BlockSpec((tm, LANES), pos_map)],
        out_specs=[tok(w) for _, w, _ in _PROJ_OUT] + [tok_t(r) for _, r in _PROJ_OUT_T],
        out_shape=([jax.ShapeDtypeStruct((groups, rows, w), dt) for _, w, dt in _PROJ_OUT]
                   + [jax.ShapeDtypeStruct((groups, r, rows), BF16) for _, r in _PROJ_OUT_T]),
        compiler_params=_cparams(2),
        name="proj_rope" if rope else "proj_ctx",
    )(x, mod3, g2, w_in, wvt, b_gate, qk_gain, ones_bd, cos, sin)
    return dict(zip([n for n, _, _ in _PROJ_OUT] + [n for n, _ in _PROJ_OUT_T], outs))


def _log_sigmoid(x):
    return jnp.minimum(x, 0.0) - jnp.log(1.0 + jnp.exp(-jnp.abs(x)))


def _split_heads(col):
    lane = lax.broadcasted_iota(jnp.int32, col.shape, 1)
    zero = jnp.zeros_like(col)
    return jnp.concatenate([jnp.where(lane < HEAD_DIM, col, zero),
                            jnp.where(lane >= HEAD_DIM, col, zero)], axis=0)


def _ret_kernel(*refs, n_chunks, with_ctx_out):
    if with_ctx_out:
        (q_ref, k_ref, v_ref, g_ref, kc_ref, vc_ref, qc_ref, gc_ref, logit_ref, gain_ref,
         o_ref, oc_ref) = refs
    else:
        q_ref, k_ref, v_ref, g_ref, kc_ref, vc_ref, logit_ref, gain_ref, o_ref = refs
    C = RET_CHUNK
    di = lax.broadcasted_iota(jnp.int32, (C, C), 0) - lax.broadcasted_iota(jnp.int32, (C, C), 1)
    dpos = jnp.maximum(di, 0).astype(F32)
    dneg = jnp.maximum(-di, 0).astype(F32)
    pos = lax.broadcasted_iota(jnp.int32, (C, 1), 0).astype(F32)
    gain = gain_ref[...]

    tabs = []
    for h in range(RET_HEADS):
        lf = _log_sigmoid(logit_ref[0:1, h:h + 1])
        lb = _log_sigmoid(logit_ref[1:2, h:h + 1])
        tabs.append(dict(
            mask=jnp.where(di >= 0, jnp.exp(lf * dpos), 0.0) + jnp.where(di <= 0, jnp.exp(lb * dneg), 0.0),
            q_f=jnp.exp(lf * (pos + 1.0)), q_b=jnp.exp(lb * (C - pos)),
            k_f=jnp.exp(lf * (C - 1.0 - pos)), k_b=jnp.exp(lb * pos),
            s_f=jnp.exp(lf * C), s_b=jnp.exp(lb * C)))

    def chunk_states(kcol, vfull, h):
        v = vfull[:, h * RET_DV:(h + 1) * RET_DV]
        t = tabs[h]
        return (_dot_tn((kcol * t['k_f']).astype(BF16), v), _dot_tn((kcol * t['k_b']).astype(BF16), v))

    def finish(o, gate):
        return (_rms(o, gain) * gate.astype(F32)).astype(BF16)

    kc_all, vc_all = kc_ref[0], vc_ref[0]
    init = [chunk_states(kc_all[:, (h // 2) * LANES:(h // 2 + 1) * LANES], vc_all, h) for h in range(RET_HEADS)]

    sums = []
    for c in range(n_chunks):
        kc_ = k_ref[0, c * C:(c + 1) * C, :]
        vc_ = v_ref[0, c * C:(c + 1) * C, :]
        sums.append([chunk_states(kc_[:, (h // 2) * LANES:(h // 2 + 1) * LANES], vc_, h) for h in range(RET_HEADS)])
    fwd = [[None] * RET_HEADS for _ in range(n_chunks)]
    bwd = [[None] * RET_HEADS for _ in range(n_chunks)]
    for h in range(RET_HEADS):
        s = init[h][0]
        for c in range(n_chunks):
            fwd[c][h] = s
            s = s * tabs[h]['s_f'] + sums[c][h][0]
        s = init[h][1]
        for c in reversed(range(n_chunks)):
            bwd[c][h] = s
            s = s * tabs[h]['s_b'] + sums[c][h][1]

    def chunk_out(q, k, v, states):
        outs = []
        for j in range(RET_HEADS // 2):
            qs = _split_heads(q[:, j * LANES:(j + 1) * LANES])
            s = _dot_nt(qs.astype(BF16), k[:, j * LANES:(j + 1) * LANES].astype(BF16))
            for hh in range(2):
                h = 2 * j + hh
                t = tabs[h]
                att = (s[hh * C:(hh + 1) * C] * t['mask']).astype(BF16)
                o = _dot(att, v[:, h * RET_DV:(h + 1) * RET_DV])
                if states is not None:
                    qm = qs[hh * C:(hh + 1) * C]
                    sf, sb = states[h]
                    o = o + _dot((qm * t['q_f']).astype(BF16), sf.astype(BF16))
                    o = o + _dot((qm * t['q_b']).astype(BF16), sb.astype(BF16))
                outs.append(o)
        return outs

    for c in range(n_chunks):
        rows = slice(c * C, (c + 1) * C)
        outs = chunk_out(q_ref[0, rows, :], k_ref[0, rows, :], v_ref[0, rows, :],
                         [(fwd[c][h], bwd[c][h]) for h in range(RET_HEADS)])
        for h in range(RET_HEADS):
            cols = slice(h * RET_DV, (h + 1) * RET_DV)
            o_ref[0, rows, cols] = finish(outs[h], g_ref[0, rows, cols])

    if with_ctx_out:
        outs = chunk_out(qc_ref[0], kc_all, vc_all, None)
        for h in range(RET_HEADS):
            cols = slice(h * RET_DV, (h + 1) * RET_DV)
            oc_ref[0, :, cols] = finish(outs[h], gc_ref[0, :, cols])


def _ret_call(pl_, pc_, logit, gain, with_ctx_out):
    b, n, _ = pl_['qb'].shape
    lc = pc_['kb'].shape[1]
    assert lc == RET_CHUNK and n % RET_CHUNK == 0
    full = lambda a: pl.BlockSpec((1,) + a.shape[1:], lambda i: (i, 0, 0))
    ins = [pl_['qb'], pl_['kb'], pl_['vb'], pl_['gb'], pc_['kb'], pc_['vb']]
    if with_ctx_out:
        ins += [pc_['qb'], pc_['gb']]
    in_specs = [full(a) for a in ins] + [_const_spec(logit.shape), _const_spec(gain.shape)]
    out_shape = [jax.ShapeDtypeStruct((b, n, RET_HEADS * RET_DV), BF16)]
    if with_ctx_out:
        out_shape.append(jax.ShapeDtypeStruct((b, lc, RET_HEADS * RET_DV), BF16))
    outs = pl.pallas_call(
        functools.partial(_ret_kernel, n_chunks=n // RET_CHUNK, with_ctx_out=with_ctx_out),
        grid=(b,),
        in_specs=in_specs,
        out_specs=[pl.BlockSpec((1,) + s.shape[1:], lambda i: (i, 0, 0)) for s in out_shape],
        out_shape=out_shape,
        compiler_params=_cparams(1),
        name="retention_ctx" if with_ctx_out else "retention",
    )(*ins, logit, gain)
    return outs if with_ctx_out else (outs[0], None)


def _fold_rows(x, op):
    r, n = x.shape
    return op(x.reshape(r // 8, 8, n), axis=0)


def _attn_kernel(*refs, has_lat, lam_init):
    if has_lat:
        (qa_ref, qc_ref, ka_ref, kc_ref, vat_ref, vct_ref, kac_ref, kcc_ref, vact_ref, vcct_ref,
         dl_ref, dg_ref, oa_ref, oc_ref, s_ref) = refs
    else:
        qa_ref, qc_ref, kac_ref, kcc_ref, vact_ref, vcct_ref, dl_ref, dg_ref, oa_ref, oc_ref, s_ref = refs

    dl = dl_ref[...]
    lam = (jnp.exp(jnp.sum(dl[0:1] * dl[1:2], axis=-1, keepdims=True))
           - jnp.exp(jnp.sum(dl[2:3] * dl[3:4], axis=-1, keepdims=True)) + lam_init)
    dgain = dg_ref[...] * (1.0 - lam_init)
    tq = min(ATTN_Q_SUB, qa_ref.shape[1])
    n_cols = DA_HEADS + GQA_HEADS // 2
    units = [(t, c) for t in range(qa_ref.shape[1] // tq) for c in range(n_cols)]
    t_lat = ka_ref.shape[1] if has_lat else 0
    t_ctx = kac_ref.shape[1]
    kb = ATTN_KEY_BLOCK
    assert t_lat % kb == 0 and t_ctx % kb == 0
    blocks = [(False, r) for r in range(0, t_lat, kb)] + [(True, r) for r in range(0, t_ctx, kb)]

    def column(u):
        t, c = units[u]
        qrows = slice(t * tq, (t + 1) * tq)
        if c < DA_HEADS:
            cols = slice(c * LANES, (c + 1) * LANES)
            return (qa_ref[0, qrows, cols], (ka_ref if has_lat else None, kac_ref), cols,
                    (vat_ref if has_lat else None, vact_ref), slice(c * DA_VROWS, (c + 1) * DA_VROWS), LANES)
        j = c - DA_HEADS
        g = j // 2
        return (qc_ref[0, qrows, slice(j * LANES, (j + 1) * LANES)], (kc_ref if has_lat else None, kcc_ref),
                slice(g * LANES, (g + 1) * LANES),
                (vct_ref if has_lat else None, vcct_ref), slice(g * GQA_VROWS, (g + 1) * GQA_VROWS), HEAD_DIM)

    runtime_zero = jnp.minimum(pl.program_id(1), 0)

    def score_block(u, qs, i):
        _, krefs, kcols, _, _, _ = column(u)
        is_ctx, r = blocks[i]
        s = _dot(krefs[is_ctx][0, r:r + kb, kcols], qs)
        s_ref[u % 2 + runtime_zero, i * kb:(i + 1) * kb, :] = s
        return _fold_rows(s, jnp.max)

    def value_block(u, m, i):
        _, _, _, vrefs, vrows, _ = column(u)
        is_ctx, r = blocks[i]
        e = jnp.exp2(s_ref[u % 2 + runtime_zero, i * kb:(i + 1) * kb, :] - m)
        return _dot(vrefs[is_ctx][0, vrows, r:r + kb], e.astype(BF16))

    def finish(u, o_ext):
        t, c = units[u]
        qrows = slice(t * tq, (t + 1) * tq)
        dv = column(u)[5]
        o = o_ext[:dv] * (1.0 / o_ext[dv:dv + 1])
        o1, o2 = o[:, :tq], o[:, tq:]
        if c < DA_HEADS:
            d = o1 - lam * o2
            d = d * lax.rsqrt(jnp.mean(d * d, axis=0, keepdims=True) + EPS)
            oa_ref[0, qrows, c * LANES:(c + 1) * LANES] = (d.T * dgain).astype(BF16)
        else:
            j = c - DA_HEADS
            oc_ref[0, qrows, j * LANES:(j + 1) * LANES] = jnp.concatenate([o1, o2], axis=0).T.astype(BF16)

    def tree(op, xs):
        return functools.reduce(op, xs)

    def split_heads_t(qcol):
        qt = qcol.astype(F32).T
        row = lax.broadcasted_iota(jnp.int32, qt.shape, 0)
        return jnp.concatenate([jnp.where(row < HEAD_DIM, qt, 0.0), jnp.where(row >= HEAD_DIM, qt, 0.0)],
                               axis=1).astype(BF16)

    qs = split_heads_t(column(0)[0])
    m = jnp.max(tree(jnp.maximum, [score_block(0, qs, i) for i in range(len(blocks))]), axis=0, keepdims=True)
    for u in range(len(units)):
        nxt = u + 1 < len(units)
        if nxt:
            qs_next = split_heads_t(column(u + 1)[0])
        maxes, outs = [], []
        for i in range(len(blocks)):
            if nxt:
                maxes.append(score_block(u + 1, qs_next, i))
            outs.append(value_block(u, m, i))
        finish(u, tree(lambda a, b: a + b, outs))
        if nxt:
            m = jnp.max(tree(jnp.maximum, maxes), axis=0, keepdims=True)


def _attn_call(pl_, pc_, batch, ctx_len, diff_lambda, diff_gain, lam_init, tq):
    has_lat = pl_ is not None
    ctx_tok = lambda w: pl.BlockSpec((1, ctx_len, w), lambda i, j: (0, i, 0))
    ctx_t = lambda r: pl.BlockSpec((1, r, ctx_len), lambda i, j: (0, 0, i))
    if has_lat:
        n = pl_['qa'].shape[1]
        qspec = lambda w: pl.BlockSpec((1, tq, w), lambda i, j: (i, j, 0))
        whole = lambda a: pl.BlockSpec((1,) + a.shape[1:], lambda i, j: (i, 0, 0))
        kvl = [pl_['ka'], pl_['kc'], pl_['vat'], pl_['vct']]
        ins = [pl_['qa'], pl_['qc']] + kvl
        in_specs = [qspec(512), qspec(512)] + [whole(a) for a in kvl]
        grid = (batch, n // tq)
        out_rows, t_total = (batch, n), n + ctx_len
    else:
        assert tq == ctx_len
        qspec = ctx_tok
        ins = [pc_['qa'], pc_['qc']]
        in_specs = [qspec(512), qspec(512)]
        grid = (batch, 1)
        out_rows, t_total = (1, batch * ctx_len), ctx_len
    ins += [pc_['ka'], pc_['kc'], pc_['vat'], pc_['vct'], diff_lambda, diff_gain]
    in_specs += [ctx_tok(512), ctx_tok(256), ctx_t(DA_HEADS * DA_VROWS), ctx_t(GQA_KV * GQA_VROWS),
                 _const_spec(diff_lambda.shape), _const_spec(diff_gain.shape)]
    return pl.pallas_call(
        functools.partial(_attn_kernel, has_lat=has_lat, lam_init=lam_init),
        grid=grid,
        in_specs=in_specs,
        out_specs=[qspec(BRANCH_W), qspec(BRANCH_W)],
        out_shape=[jax.ShapeDtypeStruct(out_rows + (BRANCH_W,), BF16)] * 2,
        scratch_shapes=[pltpu.VMEM((2, t_total, 2 * min(ATTN_Q_SUB, tq)), F32)],
        compiler_params=_cparams(2),
        name="attn_lat" if has_lat else "attn_ctx",
    )(*ins)


def _merge_kernel(h_ref, mod_ref, g_ref, oa_ref, ob_ref, oc_ref, gt_ref, wb_ref, wo_ref, o_ref):
    merged = None
    for i, br in enumerate((oa_ref, ob_ref, oc_ref)):
        p = _dot(br[0], wb_ref[i]) * gt_ref[0, :, i * D_MODEL:(i + 1) * D_MODEL].astype(F32)
        merged = p if merged is None else merged + p
    out = _dot(merged.astype(BF16), wo_ref[...])
    o_ref[0] = h_ref[0] + mod_ref[0, 2:3, :] * _rms(out, g_ref[1:2, :])


def _merge_call(h, mod3, g2, oa, ob, oc, gt, w_branch, w_out, layer, tm):
    groups, rows, d = h.shape
    tok = lambda w: pl.BlockSpec((1, tm, w), lambda g, i: (g, i, 0))
    return pl.pallas_call(
        _merge_kernel,
        grid=(groups, rows // tm),
        in_specs=[tok(d), pl.BlockSpec((1, 3, d), lambda g, i: (g, 0, 0)), _const_spec(g2.shape),
                  tok(BRANCH_W), tok(BRANCH_W), tok(BRANCH_W), tok(N_BRANCH * d),
                  _layer_spec(w_branch, (layer,)), _layer_spec(w_out, (layer,))],
        out_specs=tok(d),
        out_shape=jax.ShapeDtypeStruct(h.shape, F32),
        compiler_params=_cparams(2),
        name="merge",
    )(h, mod3, g2, oa, ob, oc, gt, w_branch, w_out)


def _rope_tables(n_tok):
    t = np.arange(n_tok)
    n_freq = ROPE_DIM // 4
    inv = ROPE_THETA ** (-np.arange(n_freq, dtype=np.float64) / n_freq)
    ang = np.concatenate([(t // GRID_W)[:, None] * inv, (t % GRID_W)[:, None] * inv], axis=-1)
    cos, sin = np.cos(ang), np.sin(ang)
    cos128 = np.tile(cos, (1, LANES // (ROPE_DIM // 2)))
    sin128 = np.tile(np.concatenate([-sin, sin], axis=-1), (1, LANES // ROPE_DIM))
    return jnp.asarray(cos128, F32), jnp.asarray(sin128, F32)


def _ones_block_diag():
    i = np.arange(LANES)
    return jnp.asarray((i[:, None] // HEAD_DIM) == (i[None, :] // HEAD_DIM), BF16)


def kernel(x, c, ctx, c_ctx, w_mod, b_mod, norm_g, w_ffn_in, w_ffn_out, w_in, b_gate, diff_lambda,
           diff_norm_g, ret_decay_logit, ret_norm_g, qk_norm_g, w_branch, w_out):
    batch, seq, d = x.shape
    ctx_len = ctx.shape[1]
    depth = w_mod.shape[0]
    assert d == D_MODEL and seq % TOKEN_TILE == 0 and (batch * ctx_len) % TOKEN_TILE == 0

    pad = (-(batch + 1)) % 8
    c_all = jnp.concatenate([c, c_ctx[None, :], jnp.zeros((pad, d), F32)], axis=0)
    mod_all = _mod_call(c_all, w_mod, b_mod).reshape(depth, batch + 1 + pad, N_SUB, 3, d)

    wi = w_ffn_in.astype(BF16)
    wo = w_ffn_out.astype(BF16)
    w_in_b = w_in.astype(BF16)
    wb = w_branch.astype(BF16)
    wout = w_out.astype(BF16)
    wvt = _wvt_call(w_in)
    bg = b_gate[:, None, :]
    cos, sin = _rope_tables(seq)
    ones_bd = _ones_block_diag()
    qk_gain = jnp.tile(qk_norm_g, (1, 1, LANES // HEAD_DIM))

    h = x
    hc = ctx.reshape(1, batch * ctx_len, d)
    for l in range(depth):
        last = l == depth - 1
        lam_init = 0.8 - 0.6 * math.exp(-0.3 * l)
        mod_l = lambda s: mod_all[l, :batch, s]
        mod_c = lambda s: mod_all[l, batch:batch + 1, s]
        ng = norm_g[l]

        h = _ffn_call(h, mod_l(0), ng[0:2], wi, wo, (l, 0), TOKEN_TILE)
        hc = _ffn_call(hc, mod_c(0), ng[0:2], wi, wo, (l, 0), TOKEN_TILE)

        p_lat = _proj_call(h, mod_l(1), ng[2:4], w_in_b, wvt, bg, l, qk_gain[l], ones_bd, cos, sin,
                           TOKEN_TILE, True)
        p_ctx = _proj_call(hc, mod_c(1), ng[2:4], w_in_b, wvt, bg, l, qk_gain[l], ones_bd, cos, sin,
                           TOKEN_TILE, False)
        per_batch = lambda a: a.reshape(batch, ctx_len, a.shape[-1])
        p_ctx_b = {k: per_batch(p_ctx[k]) for k in ('qb', 'kb', 'vb', 'gb')}

        ob, ob_c = _ret_call(p_lat, p_ctx_b, ret_decay_logit[l], ret_norm_g[l][None, :], not last)
        dgain = diff_norm_g[l][None, :]
        oa, oc = _attn_call(p_lat, p_ctx, batch, ctx_len, diff_lambda[l], dgain, lam_init, ATTN_Q_TILE)
        h = _merge_call(h, mod_l(1), ng[2:4], oa, ob, oc, p_lat['gt'], wb, wout, l, TOKEN_TILE)
        if not last:
            oa_c, oc_c = _attn_call(None, p_ctx, batch, ctx_len, diff_lambda[l], dgain, lam_init, ctx_len)
            hc = _merge_call(hc, mod_c(1), ng[2:4], oa_c, ob_c.reshape(1, batch * ctx_len, BRANCH_W), oc_c,
                             p_ctx['gt'], wb, wout, l, TOKEN_TILE)

        h = _ffn_call(h, mod_l(2), ng[4:6], wi, wo, (l, 1), TOKEN_TILE)
        if not last:
            hc = _ffn_call(hc, mod_c(2), ng[4:6], wi, wo, (l, 1), TOKEN_TILE)
    return h
```

```python
import functools
import math

import numpy as np
import jax
import jax.numpy as jnp
from jax import lax
from jax.experimental import pallas as pl
from jax.experimental.pallas import tpu as pltpu

F32 = jnp.float32
BF16 = jnp.bfloat16

D_MODEL = 1024
N_SUB = 3
D_FF = 2816
EPS = 1e-6
ROPE_THETA = 10000.0
ROPE_DIM = 64
GRID_W = 64
HEAD_DIM = 64
LANES = 128
DA_HEADS = 4
RET_HEADS = 4
RET_DV = 128
GQA_HEADS = 8
GQA_KV = 2
N_BRANCH = 3
BRANCH_W = 512
QK_SCALE = HEAD_DIM ** -0.5
SOFTMAX_Q_SCALE = QK_SCALE * math.log2(math.e)
MXU_TILE = 256
RET_CHUNK = MXU_TILE
ATTN_KEY_BLOCK = MXU_TILE
TOKEN_SUB = 2 * MXU_TILE
TOKEN_TILE = 2 * TOKEN_SUB
PROJ_TILE = 2 * MXU_TILE
ATTN_Q_SUB = MXU_TILE
ATTN_Q_TILE = 2 * ATTN_Q_SUB
PROJ_ROW_SUB = MXU_TILE
ONES_ROWS = 16
DA_VROWS = LANES + ONES_ROWS
GQA_VROWS = HEAD_DIM + ONES_ROWS
VMEM_LIMIT = 56 * 1024 * 1024

_SIZES = (512, 512, 512, 256, 256, 512, 512, 512, 128, 128, N_BRANCH * D_MODEL)
_NAMES = ('a_q', 'a_k', 'a_v', 'b_q', 'b_k', 'b_v', 'b_g', 'c_q', 'c_k', 'c_v', 'gate')
_SPAN = {}
_lo = 0
for _n, _s in zip(_NAMES, _SIZES):
    _SPAN[_n] = (_lo, _lo + _s)
    _lo += _s
IN_COLS = _lo


def _cparams(n_axes):
    return pltpu.CompilerParams(dimension_semantics=("arbitrary",) * n_axes,
                                vmem_limit_bytes=VMEM_LIMIT)


def _const_spec(shape):
    nd = len(shape)
    return pl.BlockSpec(shape, lambda *_: (0,) * nd, pipeline_mode=pl.Buffered(1))


def _rms(x, g):
    return x * lax.rsqrt(jnp.mean(x * x, axis=-1, keepdims=True) + EPS) * g


def _sigmoid(x):
    return 1.0 / (1.0 + jnp.exp(-x))


def _dot(a, b):
    return jnp.dot(a, b, preferred_element_type=F32)


def _dot_nt(a, b):
    return lax.dot_general(a, b, (((1,), (1,)), ((), ())), preferred_element_type=F32)


def _dot_tn(a, b):
    return lax.dot_general(a, b, (((0,), (0,)), ((), ())), preferred_element_type=F32)


def _mod_kernel(c_ref, w_ref, b_ref, o_ref):
    c = c_ref[...]
    x = (c * _sigmoid(c)).astype(BF16)
    o_ref[0] = _dot(x, w_ref[0].astype(BF16)) + b_ref[0]


def _mod_call(c_all, w_mod, b_mod):
    depth, d, cols = w_mod.shape
    rows = c_all.shape[0]
    tn = 1536
    return pl.pallas_call(
        _mod_kernel,
        grid=(depth, cols // tn),
        in_specs=[pl.BlockSpec((rows, d), lambda l, j: (0, 0)),
                  pl.BlockSpec((1, d, tn), lambda l, j: (l, 0, j)),
                  pl.BlockSpec((1, 1, tn), lambda l, j: (l, 0, j))],
        out_specs=pl.BlockSpec((1, rows, tn), lambda l, j: (l, 0, j)),
        out_shape=jax.ShapeDtypeStruct((depth, rows, cols), F32),
        compiler_params=_cparams(2),
        name="mod",
    )(c_all, w_mod, b_mod.reshape(depth, 1, cols))


def _wvt_kernel(a_ref, c_ref, o_ref):
    n_a = a_ref.shape[1]
    o_ref[0:n_a, :] = a_ref[...].T.astype(BF16)
    o_ref[n_a:, :] = c_ref[...].T.astype(BF16)


def _wvt_call(w_in):
    depth, d, _ = w_in.shape
    (a_lo, a_hi), (c_lo, c_hi) = _SPAN['a_v'], _SPAN['c_v']
    n_a, n_c = a_hi - a_lo, c_hi - c_lo
    assert a_lo % n_a == 0 and c_lo % n_c == 0
    return pl.pallas_call(
        _wvt_kernel,
        grid=(depth,),
        in_specs=[pl.BlockSpec((None, d, n_a), lambda l: (l, 0, a_lo // n_a)),
                  pl.BlockSpec((None, d, n_c), lambda l: (l, 0, c_lo // n_c))],
        out_specs=pl.BlockSpec((None, n_a + n_c, d), lambda l: (l, 0, 0)),
        out_shape=jax.ShapeDtypeStruct((depth, n_a + n_c, d), BF16),
        compiler_params=_cparams(1),
        name="wvt",
    )(w_in, w_in)


def _row_blocks(tm, sub):
    return [slice(r, r + sub) for r in range(0, tm, sub)]


def _ffn_kernel(x_ref, mod_ref, g_ref, wi_ref, wo_ref, o_ref):
    shift, scale, gate = mod_ref[0, 0:1, :], mod_ref[0, 1:2, :], mod_ref[0, 2:3, :]
    blocks = _row_blocks(x_ref.shape[1], TOKEN_SUB)
    ys = [(_rms(x_ref[0, rows, :], g_ref[0:1, :]) * (1.0 + scale) + shift).astype(BF16) for rows in blocks]
    for rows, y in zip(blocks, ys):
        x = x_ref[0, rows, :]
        acc = None
        for c in range(D_FF // MXU_TILE):
            lo = c * MXU_TILE
            a = _dot(y, wi_ref[:, lo:lo + MXU_TILE])
            u = _dot(y, wi_ref[:, D_FF + lo:D_FF + lo + MXU_TILE])
            hid = (a * _sigmoid(a) * u).astype(BF16)
            part = _dot(hid, wo_ref[lo:lo + MXU_TILE, :])
            acc = part if acc is None else acc + part
        o_ref[0, rows, :] = x + 0.5 * gate * _rms(acc, g_ref[1:2, :])


def _layer_spec(stacked, lead):
    shape = stacked.shape[len(lead):]
    nd = len(shape)
    return pl.BlockSpec((None,) * len(lead) + shape, lambda *_: tuple(lead) + (0,) * nd,
                        pipeline_mode=pl.Buffered(1))


def _ffn_call(x, mod3, g2, wi, wo, lead, tm):
    groups, rows, d = x.shape
    return pl.pallas_call(
        _ffn_kernel,
        grid=(groups, rows // tm),
        in_specs=[pl.BlockSpec((1, tm, d), lambda g, i: (g, i, 0)),
                  pl.BlockSpec((1, 3, d), lambda g, i: (g, 0, 0)),
                  _const_spec(g2.shape), _layer_spec(wi, lead), _layer_spec(wo, lead)],
        out_specs=pl.BlockSpec((1, tm, d), lambda g, i: (g, i, 0)),
        out_shape=jax.ShapeDtypeStruct(x.shape, F32),
        compiler_params=_cparams(2),
        name="ffn",
    )(x, mod3, g2, wi, wo)


def _swap_half_heads(x):
    lane = lax.broadcasted_iota(jnp.int32, x.shape, 1)
    return jnp.where((lane & (HEAD_DIM - 1)) < HEAD_DIM // 2,
                     pltpu.roll(x, LANES - HEAD_DIM // 2, axis=1),
                     pltpu.roll(x, HEAD_DIM // 2, axis=1))


def _rope_cols(p, cos, sin):
    cols = []
    for j in range(p.shape[1] // LANES):
        x = p[:, j * LANES:(j + 1) * LANES]
        cols.append(x * cos + _swap_half_heads(x) * sin)
    return jnp.concatenate(cols, axis=1) if len(cols) > 1 else cols[0]


def _head_rms_cols(p, gain, ones_bd):
    cols = []
    for j in range(p.shape[1] // LANES):
        x = p[:, j * LANES:(j + 1) * LANES]
        ms = _dot((x * x).astype(BF16), ones_bd) * (1.0 / HEAD_DIM)
        cols.append(x * lax.rsqrt(ms + EPS) * gain)
    return jnp.concatenate(cols, axis=1) if len(cols) > 1 else cols[0]


def _dup_heads(x):
    lane = lax.broadcasted_iota(jnp.int32, x.shape, 1)
    r = pltpu.roll(x, HEAD_DIM, axis=1)
    low = lane < HEAD_DIM
    return jnp.concatenate([jnp.where(low, x, r), jnp.where(low, r, x)], axis=1)


def _proj_kernel(x_ref, mod_ref, g_ref, w_ref, wvt_ref, bg_ref, qkg_ref, bd_ref, cos_ref, sin_ref,
                 qa_ref, ka_ref, qb_ref, kb_ref, vb_ref, gb_ref, qc_ref, kc_ref, gt_ref, vat_ref, vct_ref,
                 *, rope):
    shift, scale = mod_ref[0, 0:1, :], mod_ref[0, 1:2, :]
    bd = bd_ref[...]
    ones = jnp.ones((ONES_ROWS, PROJ_ROW_SUB), BF16)
    blocks = _row_blocks(x_ref.shape[1], PROJ_ROW_SUB)
    ys = [(_rms(x_ref[0, rows, :], g_ref[0:1, :]) * (1.0 + scale) + shift).astype(BF16) for rows in blocks]
    for rows, y in zip(blocks, ys):
        cos, sin = cos_ref[rows, :], sin_ref[rows, :]

        def proj(name):
            lo, hi = _SPAN[name]
            return _dot(y, w_ref[:, lo:hi])

        def pos(p):
            return _rope_cols(p, cos, sin) if rope else p

        cq, ck = proj('c_q'), proj('c_k')
        qa_ref[0, rows, :] = (pos(proj('a_q')) * SOFTMAX_Q_SCALE).astype(BF16)
        ka_ref[0, rows, :] = pos(proj('a_k')).astype(BF16)
        qc_ref[0, rows, :] = (pos(_head_rms_cols(cq, qkg_ref[0:1, :], bd)) * SOFTMAX_Q_SCALE).astype(BF16)
        kc_ref[0, rows, :] = _dup_heads(pos(_head_rms_cols(ck, qkg_ref[1:2, :], bd))).astype(BF16)
        qb_ref[0, rows, :] = pos(proj('b_q')) * QK_SCALE
        kb_ref[0, rows, :] = pos(proj('b_k'))
        vb_ref[0, rows, :] = proj('b_v').astype(BF16)
        g = proj('b_g')
        gb_ref[0, rows, :] = (g * _sigmoid(g)).astype(BF16)
        glo = _SPAN['gate'][0]
        for i in range(N_BRANCH):
            lo = glo + i * D_MODEL
            pre = _dot(y, w_ref[:, lo:lo + D_MODEL]) + bg_ref[:, i * D_MODEL:(i + 1) * D_MODEL]
            gt_ref[0, rows, i * D_MODEL:(i + 1) * D_MODEL] = _sigmoid(pre).astype(BF16)
        vt = _dot_nt(wvt_ref[...], y).astype(BF16)
        for c in range(DA_HEADS):
            vat_ref[0, c * DA_VROWS:c * DA_VROWS + LANES, rows] = vt[c * LANES:(c + 1) * LANES]
            vat_ref[0, c * DA_VROWS + LANES:(c + 1) * DA_VROWS, rows] = ones
        for g_ in range(GQA_KV):
            src = DA_HEADS * LANES + g_ * HEAD_DIM
            vct_ref[0, g_ * GQA_VROWS:g_ * GQA_VROWS + HEAD_DIM, rows] = vt[src:src + HEAD_DIM]
            vct_ref[0, g_ * GQA_VROWS + HEAD_DIM:(g_ + 1) * GQA_VROWS, rows] = ones


_PROJ_OUT = (('qa', 512, BF16), ('ka', 512, BF16), ('qb', 256, F32), ('kb', 256, F32),
             ('vb', 512, BF16), ('gb', 512, BF16), ('qc', 512, BF16), ('kc', 256, BF16),
             ('gt', N_BRANCH * D_MODEL, BF16))
_PROJ_OUT_T = (('vat', DA_HEADS * DA_VROWS), ('vct', GQA_KV * GQA_VROWS))


def _proj_call(x, mod3, g2, w_in, wvt, b_gate, layer, qk_gain, ones_bd, cos, sin, tm, rope):
    groups, rows, d = x.shape
    tok = lambda w: pl.BlockSpec((1, tm, w), lambda g, i: (g, i, 0))
    tok_t = lambda r: pl.BlockSpec((1, r, tm), lambda g, i: (g, 0, i))
    pos_map = (lambda g, i: (i, 0)) if rope else (lambda g, i: (0, 0))
    outs = pl.pallas_call(
        functools.partial(_proj_kernel, rope=rope),
        grid=(groups, rows // tm),
        in_specs=[tok(d),
                  pl.BlockSpec((1, 3, d), lambda g, i: (g, 0, 0)),
                  _const_spec(g2.shape), _layer_spec(w_in, (layer,)), _layer_spec(wvt, (layer,)),
                  _layer_spec(b_gate, (layer,)),
                  _const_spec(qk_gain.shape), _const_spec(ones_bd.shape),
                  pl.BlockSpec((tm, LANES), pos_map), pl.BlockSpec((tm, LANES), pos_map)],
        out_specs=[tok(w) for _, w, _ in _PROJ_OUT] + [tok_t(r) for _, r in _PROJ_OUT_T],
        out_shape=([jax.ShapeDtypeStruct((groups, rows, w), dt) for _, w, dt in _PROJ_OUT]
                   + [jax.ShapeDtypeStruct((groups, r, rows), BF16) for _, r in _PROJ_OUT_T]),
        compiler_params=_cparams(2),
        name="proj_rope" if rope else "proj_ctx",
    )(x, mod3, g2, w_in, wvt, b_gate, qk_gain, ones_bd, cos, sin)
    return dict(zip([n for n, _, _ in _PROJ_OUT] + [n for n, _ in _PROJ_OUT_T], outs))


def _log_sigmoid(x):
    return jnp.minimum(x, 0.0) - jnp.log(1.0 + jnp.exp(-jnp.abs(x)))


def _split_heads(col):
    lane = lax.broadcasted_iota(jnp.int32, col.shape, 1)
    zero = jnp.zeros_like(col)
    return jnp.concatenate([jnp.where(lane < HEAD_DIM, col, zero),
                            jnp.where(lane >= HEAD_DIM, col, zero)], axis=0)


def _ret_kernel(*refs, n_chunks, with_ctx_out):
    if with_ctx_out:
        (q_ref, k_ref, v_ref, g_ref, kc_ref, vc_ref, qc_ref, gc_ref, logit_ref, gain_ref,
         o_ref, oc_ref) = refs
    else:
        q_ref, k_ref, v_ref, g_ref, kc_ref, vc_ref, logit_ref, gain_ref, o_ref = refs
    C = RET_CHUNK
    di = lax.broadcasted_iota(jnp.int32, (C, C), 0) - lax.broadcasted_iota(jnp.int32, (C, C), 1)
    dpos = jnp.maximum(di, 0).astype(F32)
    dneg = jnp.maximum(-di, 0).astype(F32)
    pos = lax.broadcasted_iota(jnp.int32, (C, 1), 0).astype(F32)
    gain = gain_ref[...]

    tabs = []
    for h in range(RET_HEADS):
        lf = _log_sigmoid(logit_ref[0:1, h:h + 1])
        lb = _log_sigmoid(logit_ref[1:2, h:h + 1])
        tabs.append(dict(
            mask=jnp.where(di >= 0, jnp.exp(lf * dpos), 0.0) + jnp.where(di <= 0, jnp.exp(lb * dneg), 0.0),
            q_f=jnp.exp(lf * (pos + 1.0)), q_b=jnp.exp(lb * (C - pos)),
            k_f=jnp.exp(lf * (C - 1.0 - pos)), k_b=jnp.exp(lb * pos),
            s_f=jnp.exp(lf * C), s_b=jnp.exp(lb * C)))

    def chunk_states(kcol, vfull, h):
        v = vfull[:, h * RET_DV:(h + 1) * RET_DV]
        t = tabs[h]
        return (_dot_tn((kcol * t['k_f']).astype(BF16), v), _dot_tn((kcol * t['k_b']).astype(BF16), v))

    def finish(o, gate):
        return (_rms(o, gain) * gate.astype(F32)).astype(BF16)

    kc_all, vc_all = kc_ref[0], vc_ref[0]
    init = [chunk_states(kc_all[:, (h // 2) * LANES:(h // 2 + 1) * LANES], vc_all, h) for h in range(RET_HEADS)]

    sums = []
    for c in range(n_chunks):
        kc_ = k_ref[0, c * C:(c + 1) * C, :]
        vc_ = v_ref[0, c * C:(c + 1) * C, :]
        sums.append([chunk_states(kc_[:, (h // 2) * LANES:(h // 2 + 1) * LANES], vc_, h) for h in range(RET_HEADS)])
    fwd = [[None] * RET_HEADS for _ in range(n_chunks)]
    bwd = [[None] * RET_HEADS for _ in range(n_chunks)]
    for h in range(RET_HEADS):
        s = init[h][0]
        for c in range(n_chunks):
            fwd[c][h] = s
            s = s * tabs[h]['s_f'] + sums[c][h][0]
        s = init[h][1]
        for c in reversed(range(n_chunks)):
            bwd[c][h] = s
            s = s * tabs[h]['s_b'] + sums[c][h][1]

    def chunk_out(q, k, v, states):
        outs = []
        for j in range(RET_HEADS // 2):
            qs = _split_heads(q[:, j * LANES:(j + 1) * LANES])
            s = _dot_nt(qs.astype(BF16), k[:, j * LANES:(j + 1) * LANES].astype(BF16))
            for hh in range(2):
                h = 2 * j + hh
                t = tabs[h]
                att = (s[hh * C:(hh + 1) * C] * t['mask']).astype(BF16)
                o = _dot(att, v[:, h * RET_DV:(h + 1) * RET_DV])
                if states is not None:
                    qm = qs[hh * C:(hh + 1) * C]
                    sf, sb = states[h]
                    o = o + _dot((qm * t['q_f']).astype(BF16), sf.astype(BF16))
                    o = o + _dot((qm * t['q_b']).astype(BF16), sb.astype(BF16))
                outs.append(o)
        return outs

    for c in range(n_chunks):
        rows = slice(c * C, (c + 1) * C)
        outs = chunk_out(q_ref[0, rows, :], k_ref[0, rows, :], v_ref[0, rows, :],
                         [(fwd[c][h], bwd[c][h]) for h in range(RET_HEADS)])
        for h in range(RET_HEADS):
            cols = slice(h * RET_DV, (h + 1) * RET_DV)
            o_ref[0, rows, cols] = finish(outs[h], g_ref[0, rows, cols])

    if with_ctx_out:
        outs = chunk_out(qc_ref[0], kc_all, vc_all, None)
        for h in range(RET_HEADS):
            cols = slice(h * RET_DV, (h + 1) * RET_DV)
            oc_ref[0, :, cols] = finish(outs[h], gc_ref[0, :, cols])


def _ret_call(pl_, pc_, logit, gain, with_ctx_out):
    b, n, _ = pl_['qb'].shape
    lc = pc_['kb'].shape[1]
    assert lc == RET_CHUNK and n % RET_CHUNK == 0
    full = lambda a: pl.BlockSpec((1,) + a.shape[1:], lambda i: (i, 0, 0))
    ins = [pl_['qb'], pl_['kb'], pl_['vb'], pl_['gb'], pc_['kb'], pc_['vb']]
    if with_ctx_out:
        ins += [pc_['qb'], pc_['gb']]
    in_specs = [full(a) for a in ins] + [_const_spec(logit.shape), _const_spec(gain.shape)]
    out_shape = [jax.ShapeDtypeStruct((b, n, RET_HEADS * RET_DV), BF16)]
    if with_ctx_out:
        out_shape.append(jax.ShapeDtypeStruct((b, lc, RET_HEADS * RET_DV), BF16))
    outs = pl.pallas_call(
        functools.partial(_ret_kernel, n_chunks=n // RET_CHUNK, with_ctx_out=with_ctx_out),
        grid=(b,),
        in_specs=in_specs,
        out_specs=[pl.BlockSpec((1,) + s.shape[1:], lambda i: (i, 0, 0)) for s in out_shape],
        out_shape=out_shape,
        compiler_params=_cparams(1),
        name="retention_ctx" if with_ctx_out else "retention",
    )(*ins, logit, gain)
    return outs if with_ctx_out else (outs[0], None)


def _fold_rows(x, op):
    r, n = x.shape
    return op(x.reshape(r // 8, 8, n), axis=0)


def _attn_kernel(*refs, has_lat, lam_init):
    if has_lat:
        (qa_ref, qc_ref, ka_ref, kc_ref, vat_ref, vct_ref, kac_ref, kcc_ref, vact_ref, vcct_ref,
         dl_ref, dg_ref, oa_ref, oc_ref, s_ref) = refs
    else:
        qa_ref, qc_ref, kac_ref, kcc_ref, vact_ref, vcct_ref, dl_ref, dg_ref, oa_ref, oc_ref, s_ref = refs

    dl = dl_ref[...]
    lam = (jnp.exp(jnp.sum(dl[0:1] * dl[1:2], axis=-1, keepdims=True))
           - jnp.exp(jnp.sum(dl[2:3] * dl[3:4], axis=-1, keepdims=True)) + lam_init)
    dgain = dg_ref[...] * (1.0 - lam_init)
    tq = min(ATTN_Q_SUB, qa_ref.shape[1])
    n_cols = DA_HEADS + GQA_HEADS // 2
    units = [(t, c) for t in range(qa_ref.shape[1] // tq) for c in range(n_cols)]
    t_lat = ka_ref.shape[1] if has_lat else 0
    t_ctx = kac_ref.shape[1]
    kb = ATTN_KEY_BLOCK
    assert t_lat % kb == 0 and t_ctx % kb == 0
    blocks = [(False, r) for r in range(0, t_lat, kb)] + [(True, r) for r in range(0, t_ctx, kb)]

    def column(u):
        t, c = units[u]
        qrows = slice(t * tq, (t + 1) * tq)
        if c < DA_HEADS:
            cols = slice(c * LANES, (c + 1) * LANES)
            return (qa_ref[0, qrows, cols], (ka_ref if has_lat else None, kac_ref), cols,
                    (vat_ref if has_lat else None, vact_ref), slice(c * DA_VROWS, (c + 1) * DA_VROWS), LANES)
        j = c - DA_HEADS
        g = j // 2
        return (qc_ref[0, qrows, slice(j * LANES, (j + 1) * LANES)], (kc_ref if has_lat else None, kcc_ref),
                slice(g * LANES, (g + 1) * LANES),
                (vct_ref if has_lat else None, vcct_ref), slice(g * GQA_VROWS, (g + 1) * GQA_VROWS), HEAD_DIM)

    runtime_zero = jnp.minimum(pl.program_id(1), 0)

    def score_block(u, qs, i):
        _, krefs, kcols, _, _, _ = column(u)
        is_ctx, r = blocks[i]
        s = _dot(krefs[is_ctx][0, r:r + kb, kcols], qs)
        s_ref[u % 2 + runtime_zero, i * kb:(i + 1) * kb, :] = s
        return _fold_rows(s, jnp.max)

    def value_block(u, m, i):
        _, _, _, vrefs, vrows, _ = column(u)
        is_ctx, r = blocks[i]
        e = jnp.exp2(s_ref[u % 2 + runtime_zero, i * kb:(i + 1) * kb, :] - m)
        return _dot(vrefs[is_ctx][0, vrows, r:r + kb], e.astype(BF16))

    def finish(u, o_ext):
        t, c = units[u]
        qrows = slice(t * tq, (t + 1) * tq)
        dv = column(u)[5]
        o = o_ext[:dv] * (1.0 / o_ext[dv:dv + 1])
        o1, o2 = o[:, :tq], o[:, tq:]
        if c < DA_HEADS:
            d = o1 - lam * o2
            d = d * lax.rsqrt(jnp.mean(d * d, axis=0, keepdims=True) + EPS)
            oa_ref[0, qrows, c * LANES:(c + 1) * LANES] = (d.T * dgain).astype(BF16)
        else:
            j = c - DA_HEADS
            oc_ref[0, qrows, j * LANES:(j + 1) * LANES] = jnp.concatenate([o1, o2], axis=0).T.astype(BF16)

    def tree(op, xs):
        return functools.reduce(op, xs)

    def split_heads_t(qcol):
        qt = qcol.astype(F32).T
        row = lax.broadcasted_iota(jnp.int32, qt.shape, 0)
        return jnp.concatenate([jnp.where(row < HEAD_DIM, qt, 0.0), jnp.where(row >= HEAD_DIM, qt, 0.0)],
                               axis=1).astype(BF16)

    qs = split_heads_t(column(0)[0])
    m = jnp.max(tree(jnp.maximum, [score_block(0, qs, i) for i in range(len(blocks))]), axis=0, keepdims=True)
    for u in range(len(units)):
        nxt = u + 1 < len(units)
        if nxt:
            qs_next = split_heads_t(column(u + 1)[0])
        maxes, outs = [], []
        for i in range(len(blocks)):
            if nxt:
                maxes.append(score_block(u + 1, qs_next, i))
            outs.append(value_block(u, m, i))
        finish(u, tree(lambda a, b: a + b, outs))
        if nxt:
            m = jnp.max(tree(jnp.maximum, maxes), axis=0, keepdims=True)


def _attn_call(pl_, pc_, batch, ctx_len, diff_lambda, diff_gain, lam_init, tq):
    has_lat = pl_ is not None
    ctx_tok = lambda w: pl.BlockSpec((1, ctx_len, w), lambda i, j: (0, i, 0))
    ctx_t = lambda r: pl.BlockSpec((1, r, ctx_len), lambda i, j: (0, 0, i))
    if has_lat:
        n = pl_['qa'].shape[1]
        qspec = lambda w: pl.BlockSpec((1, tq, w), lambda i, j: (i, j, 0))
        whole = lambda a: pl.BlockSpec((1,) + a.shape[1:], lambda i, j: (i, 0, 0))
        kvl = [pl_['ka'], pl_['kc'], pl_['vat'], pl_['vct']]
        ins = [pl_['qa'], pl_['qc']] + kvl
        in_specs = [qspec(512), qspec(512)] + [whole(a) for a in kvl]
        grid = (batch, n // tq)
        out_rows, t_total = (batch, n), n + ctx_len
    else:
        assert tq == ctx_len
        qspec = ctx_tok
        ins = [pc_['qa'], pc_['qc']]
        in_specs = [qspec(512), qspec(512)]
        grid = (batch, 1)
        out_rows, t_total = (1, batch * ctx_len), ctx_len
    ins += [pc_['ka'], pc_['kc'], pc_['vat'], pc_['vct'], diff_lambda, diff_gain]
    in_specs += [ctx_tok(512), ctx_tok(256), ctx_t(DA_HEADS * DA_VROWS), ctx_t(GQA_KV * GQA_VROWS),
                 _const_spec(diff_lambda.shape), _const_spec(diff_gain.shape)]
    return pl.pallas_call(
        functools.partial(_attn_kernel, has_lat=has_lat, lam_init=lam_init),
        grid=grid,
        in_specs=in_specs,
        out_specs=[qspec(BRANCH_W), qspec(BRANCH_W)],
        out_shape=[jax.ShapeDtypeStruct(out_rows + (BRANCH_W,), BF16)] * 2,
        scratch_shapes=[pltpu.VMEM((2, t_total, 2 * min(ATTN_Q_SUB, tq)), F32)],
        compiler_params=_cparams(2),
        name="attn_lat" if has_lat else "attn_ctx",
    )(*ins)


def _merge_kernel(h_ref, mod_ref, g_ref, oa_ref, ob_ref, oc_ref, gt_ref, wb_ref, wo_ref, o_ref):
    for rows in _row_blocks(h_ref.shape[1], TOKEN_SUB):
        merged = None
        for i, br in enumerate((oa_ref, ob_ref, oc_ref)):
            p = _dot(br[0, rows, :], wb_ref[i]) * gt_ref[0, rows, i * D_MODEL:(i + 1) * D_MODEL].astype(F32)
            merged = p if merged is None else merged + p
        out = _dot(merged.astype(BF16), wo_ref[...])
        o_ref[0, rows, :] = h_ref[0, rows, :] + mod_ref[0, 2:3, :] * _rms(out, g_ref[1:2, :])


def _merge_call(h, mod3, g2, oa, ob, oc, gt, w_branch, w_out, layer, tm):
    groups, rows, d = h.shape
    tok = lambda w: pl.BlockSpec((1, tm, w), lambda g, i: (g, i, 0))
    return pl.pallas_call(
        _merge_kernel,
        grid=(groups, rows // tm),
        in_specs=[tok(d), pl.BlockSpec((1, 3, d), lambda g, i: (g, 0, 0)), _const_spec(g2.shape),
                  tok(BRANCH_W), tok(BRANCH_W), tok(BRANCH_W), tok(N_BRANCH * d),
                  _layer_spec(w_branch, (layer,)), _layer_spec(w_out, (layer,))],
        out_specs=tok(d),
        out_shape=jax.ShapeDtypeStruct(h.shape, F32),
        compiler_params=_cparams(2),
        name="merge",
    )(h, mod3, g2, oa, ob, oc, gt, w_branch, w_out)


def _rope_tables(n_tok):
    t = np.arange(n_tok)
    n_freq = ROPE_DIM // 4
    inv = ROPE_THETA ** (-np.arange(n_freq, dtype=np.float64) / n_freq)
    ang = np.concatenate([(t // GRID_W)[:, None] * inv, (t % GRID_W)[:, None] * inv], axis=-1)
    cos, sin = np.cos(ang), np.sin(ang)
    cos128 = np.tile(cos, (1, LANES // (ROPE_DIM // 2)))
    sin128 = np.tile(np.concatenate([-sin, sin], axis=-1), (1, LANES // ROPE_DIM))
    return jnp.asarray(cos128, F32), jnp.asarray(sin128, F32)


def _ones_block_diag():
    i = np.arange(LANES)
    return jnp.asarray((i[:, None] // HEAD_DIM) == (i[None, :] // HEAD_DIM), BF16)


def kernel(x, c, ctx, c_ctx, w_mod, b_mod, norm_g, w_ffn_in, w_ffn_out, w_in, b_gate, diff_lambda,
           diff_norm_g, ret_decay_logit, ret_norm_g, qk_norm_g, w_branch, w_out):
    batch, seq, d = x.shape
    ctx_len = ctx.shape[1]
    depth = w_mod.shape[0]
    assert d == D_MODEL and seq % TOKEN_TILE == 0 and (batch * ctx_len) % TOKEN_TILE == 0

    pad = (-(batch + 1)) % 8
    c_all = jnp.concatenate([c, c_ctx[None, :], jnp.zeros((pad, d), F32)], axis=0)
    mod_all = _mod_call(c_all, w_mod, b_mod).reshape(depth, batch + 1 + pad, N_SUB, 3, d)

    wi = w_ffn_in.astype(BF16)
    wo = w_ffn_out.astype(BF16)
    w_in_b = w_in.astype(BF16)
    wb = w_branch.astype(BF16)
    wout = w_out.astype(BF16)
    wvt = _wvt_call(w_in)
    bg = b_gate[:, None, :]
    cos, sin = _rope_tables(seq)
    ones_bd = _ones_block_diag()
    qk_gain = jnp.tile(qk_norm_g, (1, 1, LANES // HEAD_DIM))

    h = x
    hc = ctx.reshape(1, batch * ctx_len, d)
    for l in range(depth):
        last = l == depth - 1
        lam_init = 0.8 - 0.6 * math.exp(-0.3 * l)
        mod_l = lambda s: mod_all[l, :batch, s]
        mod_c = lambda s: mod_all[l, batch:batch + 1, s]
        ng = norm_g[l]

        h = _ffn_call(h, mod_l(0), ng[0:2], wi, wo, (l, 0), TOKEN_TILE)
        hc = _ffn_call(hc, mod_c(0), ng[0:2], wi, wo, (l, 0), TOKEN_TILE)

        p_lat = _proj_call(h, mod_l(1), ng[2:4], w_in_b, wvt, bg, l, qk_gain[l], ones_bd, cos, sin,
                           PROJ_TILE, True)
        p_ctx = _proj_call(hc, mod_c(1), ng[2:4], w_in_b, wvt, bg, l, qk_gain[l], ones_bd, cos, sin,
                           PROJ_TILE, False)
        per_batch = lambda a: a.reshape(batch, ctx_len, a.shape[-1])
        p_ctx_b = {k: per_batch(p_ctx[k]) for k in ('qb', 'kb', 'vb', 'gb')}

        ob, ob_c = _ret_call(p_lat, p_ctx_b, ret_decay_logit[l], ret_norm_g[l][None, :], not last)
        dgain = diff_norm_g[l][None, :]
        oa, oc = _attn_call(p_lat, p_ctx, batch, ctx_len, diff_lambda[l], dgain, lam_init, ATTN_Q_TILE)
        h = _merge_call(h, mod_l(1), ng[2:4], oa, ob, oc, p_lat['gt'], wb, wout, l, TOKEN_TILE)
        if not last:
            oa_c, oc_c = _attn_call(None, p_ctx, batch, ctx_len, diff_lambda[l], dgain, lam_init, ctx_len)
            hc = _merge_call(hc, mod_c(1), ng[2:4], oa_c, ob_c.reshape(1, batch * ctx_len, BRANCH_W), oc_c,
                             p_ctx['gt'], wb, wout, l, TOKEN_TILE)

        h = _ffn_call(h, mod_l(2), ng[4:6], wi, wo, (l, 1), TOKEN_TILE)
        if not last:
            hc = _ffn_call(hc, mod_c(2), ng[4:6], wi, wo, (l, 1), TOKEN_TILE)
    return h
```

```python
import functools
import math

import numpy as np
import jax
import jax.numpy as jnp
from jax import lax
from jax.experimental import pallas as pl
from jax.experimental.pallas import tpu as pltpu

F32 = jnp.float32
BF16 = jnp.bfloat16

D_MODEL = 1024
N_SUB = 3
D_FF = 2816
EPS = 1e-6
ROPE_THETA = 10000.0
ROPE_DIM = 64
GRID_W = 64
HEAD_DIM = 64
LANES = 128
DA_HEADS = 4
RET_HEADS = 4
RET_DV = 128
GQA_HEADS = 8
GQA_KV = 2
N_BRANCH = 3
BRANCH_W = 512
QK_SCALE = HEAD_DIM ** -0.5
SOFTMAX_Q_SCALE = QK_SCALE * math.log2(math.e)
MXU_TILE = 256
RET_CHUNK = MXU_TILE
ATTN_KEY_BLOCK = MXU_TILE
TOKEN_SUB = 2 * MXU_TILE
TOKEN_TILE = 2 * TOKEN_SUB
PROJ_TILE = 2 * MXU_TILE
ATTN_Q_SUB = MXU_TILE
ATTN_Q_TILE = 4 * ATTN_Q_SUB
PROJ_ROW_SUB = MXU_TILE
ONES_ROWS = 16
DA_VROWS = LANES + ONES_ROWS
GQA_VROWS = HEAD_DIM + ONES_ROWS
VMEM_LIMIT = 56 * 1024 * 1024

_SIZES = (512, 512, 512, 256, 256, 512, 512, 512, 128, 128, N_BRANCH * D_MODEL)
_NAMES = ('a_q', 'a_k', 'a_v', 'b_q', 'b_k', 'b_v', 'b_g', 'c_q', 'c_k', 'c_v', 'gate')
_SPAN = {}
_lo = 0
for _n, _s in zip(_NAMES, _SIZES):
    _SPAN[_n] = (_lo, _lo + _s)
    _lo += _s
IN_COLS = _lo


def _cparams(n_axes):
    return pltpu.CompilerParams(dimension_semantics=("arbitrary",) * n_axes,
                                vmem_limit_bytes=VMEM_LIMIT)


def _const_spec(shape):
    nd = len(shape)
    return pl.BlockSpec(shape, lambda *_: (0,) * nd, pipeline_mode=pl.Buffered(1))


def _rms(x, g):
    return x * lax.rsqrt(jnp.mean(x * x, axis=-1, keepdims=True) + EPS) * g


def _sigmoid(x):
    return 1.0 / (1.0 + jnp.exp(-x))


def _dot(a, b):
    return jnp.dot(a, b, preferred_element_type=F32)


def _dot_nt(a, b):
    return lax.dot_general(a, b, (((1,), (1,)), ((), ())), preferred_element_type=F32)


def _dot_tn(a, b):
    return lax.dot_general(a, b, (((0,), (0,)), ((), ())), preferred_element_type=F32)


def _mod_kernel(c_ref, w_ref, b_ref, o_ref):
    c = c_ref[...]
    x = (c * _sigmoid(c)).astype(BF16)
    o_ref[0] = _dot(x, w_ref[0].astype(BF16)) + b_ref[0]


def _mod_call(c_all, w_mod, b_mod):
    depth, d, cols = w_mod.shape
    rows = c_all.shape[0]
    tn = 1536
    return pl.pallas_call(
        _mod_kernel,
        grid=(depth, cols // tn),
        in_specs=[pl.BlockSpec((rows, d), lambda l, j: (0, 0)),
                  pl.BlockSpec((1, d, tn), lambda l, j: (l, 0, j)),
                  pl.BlockSpec((1, 1, tn), lambda l, j: (l, 0, j))],
        out_specs=pl.BlockSpec((1, rows, tn), lambda l, j: (l, 0, j)),
        out_shape=jax.ShapeDtypeStruct((depth, rows, cols), F32),
        compiler_params=_cparams(2),
        name="mod",
    )(c_all, w_mod, b_mod.reshape(depth, 1, cols))


def _wvt_kernel(a_ref, c_ref, o_ref):
    n_a = a_ref.shape[1]
    o_ref[0:n_a, :] = a_ref[...].T.astype(BF16)
    o_ref[n_a:, :] = c_ref[...].T.astype(BF16)


def _wvt_call(w_in):
    depth, d, _ = w_in.shape
    (a_lo, a_hi), (c_lo, c_hi) = _SPAN['a_v'], _SPAN['c_v']
    n_a, n_c = a_hi - a_lo, c_hi - c_lo
    assert a_lo % n_a == 0 and c_lo % n_c == 0
    return pl.pallas_call(
        _wvt_kernel,
        grid=(depth,),
        in_specs=[pl.BlockSpec((None, d, n_a), lambda l: (l, 0, a_lo // n_a)),
                  pl.BlockSpec((None, d, n_c), lambda l: (l, 0, c_lo // n_c))],
        out_specs=pl.BlockSpec((None, n_a + n_c, d), lambda l: (l, 0, 0)),
        out_shape=jax.ShapeDtypeStruct((depth, n_a + n_c, d), BF16),
        compiler_params=_cparams(1),
        name="wvt",
    )(w_in, w_in)


def _row_blocks(tm, sub):
    return [slice(r, r + sub) for r in range(0, tm, sub)]


def _ffn_kernel(x_ref, mod_ref, g_ref, wi_ref, wo_ref, o_ref):
    shift, scale, gate = mod_ref[0, 0:1, :], mod_ref[0, 1:2, :], mod_ref[0, 2:3, :]
    blocks = _row_blocks(x_ref.shape[1], TOKEN_SUB)
    ys = [(_rms(x_ref[0, rows, :], g_ref[0:1, :]) * (1.0 + scale) + shift).astype(BF16) for rows in blocks]
    for rows, y in zip(blocks, ys):
        x = x_ref[0, rows, :]
        acc = None
        for c in range(D_FF // MXU_TILE):
            lo = c * MXU_TILE
            a = _dot(y, wi_ref[:, lo:lo + MXU_TILE])
            u = _dot(y, wi_ref[:, D_FF + lo:D_FF + lo + MXU_TILE])
            hid = (a * _sigmoid(a) * u).astype(BF16)
            part = _dot(hid, wo_ref[lo:lo + MXU_TILE, :])
            acc = part if acc is None else acc + part
        o_ref[0, rows, :] = x + 0.5 * gate * _rms(acc, g_ref[1:2, :])


def _layer_spec(stacked, lead):
    shape = stacked.shape[len(lead):]
    nd = len(shape)
    return pl.BlockSpec((None,) * len(lead) + shape, lambda *_: tuple(lead) + (0,) * nd,
                        pipeline_mode=pl.Buffered(1))


def _ffn_call(x, mod3, g2, wi, wo, lead, tm):
    groups, rows, d = x.shape
    return pl.pallas_call(
        _ffn_kernel,
        grid=(groups, rows // tm),
        in_specs=[pl.BlockSpec((1, tm, d), lambda g, i: (g, i, 0)),
                  pl.BlockSpec((1, 3, d), lambda g, i: (g, 0, 0)),
                  _const_spec(g2.shape), _layer_spec(wi, lead), _layer_spec(wo, lead)],
        out_specs=pl.BlockSpec((1, tm, d), lambda g, i: (g, i, 0)),
        out_shape=jax.ShapeDtypeStruct(x.shape, F32),
        compiler_params=_cparams(2),
        name="ffn",
    )(x, mod3, g2, wi, wo)


def _swap_half_heads(x):
    lane = lax.broadcasted_iota(jnp.int32, x.shape, 1)
    return jnp.where((lane & (HEAD_DIM - 1)) < HEAD_DIM // 2,
                     pltpu.roll(x, LANES - HEAD_DIM // 2, axis=1),
                     pltpu.roll(x, HEAD_DIM // 2, axis=1))


def _rope_cols(p, cos, sin):
    cols = []
    for j in range(p.shape[1] // LANES):
        x = p[:, j * LANES:(j + 1) * LANES]
        cols.append(x * cos + _swap_half_heads(x) * sin)
    return jnp.concatenate(cols, axis=1) if len(cols) > 1 else cols[0]


def _head_rms_cols(p, gain, ones_bd):
    cols = []
    for j in range(p.shape[1] // LANES):
        x = p[:, j * LANES:(j + 1) * LANES]
        ms = _dot((x * x).astype(BF16), ones_bd) * (1.0 / HEAD_DIM)
        cols.append(x * lax.rsqrt(ms + EPS) * gain)
    return jnp.concatenate(cols, axis=1) if len(cols) > 1 else cols[0]


def _dup_heads(x):
    lane = lax.broadcasted_iota(jnp.int32, x.shape, 1)
    r = pltpu.roll(x, HEAD_DIM, axis=1)
    low = lane < HEAD_DIM
    return jnp.concatenate([jnp.where(low, x, r), jnp.where(low, r, x)], axis=1)


def _proj_kernel(x_ref, mod_ref, g_ref, w_ref, wvt_ref, bg_ref, qkg_ref, bd_ref, cos_ref, sin_ref,
                 qa_ref, ka_ref, qb_ref, kb_ref, vb_ref, gb_ref, qc_ref, kc_ref, gt_ref, vat_ref, vct_ref,
                 *, rope):
    shift, scale = mod_ref[0, 0:1, :], mod_ref[0, 1:2, :]
    bd = bd_ref[...]
    ones = jnp.ones((ONES_ROWS, PROJ_ROW_SUB), BF16)
    blocks = _row_blocks(x_ref.shape[1], PROJ_ROW_SUB)
    ys = [(_rms(x_ref[0, rows, :], g_ref[0:1, :]) * (1.0 + scale) + shift).astype(BF16) for rows in blocks]
    for rows, y in zip(blocks, ys):
        cos, sin = cos_ref[rows, :], sin_ref[rows, :]

        def proj(name):
            lo, hi = _SPAN[name]
            return _dot(y, w_ref[:, lo:hi])

        def pos(p):
            return _rope_cols(p, cos, sin) if rope else p

        cq, ck = proj('c_q'), proj('c_k')
        qa_ref[0, rows, :] = (pos(proj('a_q')) * SOFTMAX_Q_SCALE).astype(BF16)
        ka_ref[0, rows, :] = pos(proj('a_k')).astype(BF16)
        qc_ref[0, rows, :] = (pos(_head_rms_cols(cq, qkg_ref[0:1, :], bd)) * SOFTMAX_Q_SCALE).astype(BF16)
        kc_ref[0, rows, :] = _dup_heads(pos(_head_rms_cols(ck, qkg_ref[1:2, :], bd))).astype(BF16)
        qb_ref[0, rows, :] = pos(proj('b_q')) * QK_SCALE
        kb_ref[0, rows, :] = pos(proj('b_k'))
        vb_ref[0, rows, :] = proj('b_v').astype(BF16)
        g = proj('b_g')
        gb_ref[0, rows, :] = (g * _sigmoid(g)).astype(BF16)
        glo = _SPAN['gate'][0]
        for i in range(N_BRANCH):
            lo = glo + i * D_MODEL
            pre = _dot(y, w_ref[:, lo:lo + D_MODEL]) + bg_ref[:, i * D_MODEL:(i + 1) * D_MODEL]
            gt_ref[0, rows, i * D_MODEL:(i + 1) * D_MODEL] = _sigmoid(pre).astype(BF16)
        vt = _dot_nt(wvt_ref[...], y).astype(BF16)
        for c in range(DA_HEADS):
            vat_ref[0, c * DA_VROWS:c * DA_VROWS + LANES, rows] = vt[c * LANES:(c + 1) * LANES]
            vat_ref[0, c * DA_VROWS + LANES:(c + 1) * DA_VROWS, rows] = ones
        for g_ in range(GQA_KV):
            src = DA_HEADS * LANES + g_ * HEAD_DIM
            vct_ref[0, g_ * GQA_VROWS:g_ * GQA_VROWS + HEAD_DIM, rows] = vt[src:src + HEAD_DIM]
            vct_ref[0, g_ * GQA_VROWS + HEAD_DIM:(g_ + 1) * GQA_VROWS, rows] = ones


_PROJ_OUT = (('qa', 512, BF16), ('ka', 512, BF16), ('qb', 256, F32), ('kb', 256, F32),
             ('vb', 512, BF16), ('gb', 512, BF16), ('qc', 512, BF16), ('kc', 256, BF16),
             ('gt', N_BRANCH * D_MODEL, BF16))
_PROJ_OUT_T = (('vat', DA_HEADS * DA_VROWS), ('vct', GQA_KV * GQA_VROWS))


def _proj_call(x, mod3, g2, w_in, wvt, b_gate, layer, qk_gain, ones_bd, cos, sin, tm, rope):
    groups, rows, d = x.shape
    tok = lambda w: pl.BlockSpec((1, tm, w), lambda g, i: (g, i, 0))
    tok_t = lambda r: pl.BlockSpec((1, r, tm), lambda g, i: (g, 0, i))
    pos_map = (lambda g, i: (i, 0)) if rope else (lambda g, i: (0, 0))
    outs = pl.pallas_call(
        functools.partial(_proj_kernel, rope=rope),
        grid=(groups, rows // tm),
        in_specs=[tok(d),
                  pl.BlockSpec((1, 3, d), lambda g, i: (g, 0, 0)),
                  _const_spec(g2.shape), _layer_spec(w_in, (layer,)), _layer_spec(wvt, (layer,)),
                  _layer_spec(b_gate, (layer,)),
                  _const_spec(qk_gain.shape), _const_spec(ones_bd.shape),
                  pl.BlockSpec((tm, LANES), pos_map), pl.BlockSpec((tm, LANES), pos_map)],
        out_specs=[tok(w) for _, w, _ in _PROJ_OUT] + [tok_t(r) for _, r in _PROJ_OUT_T],
        out_shape=([jax.ShapeDtypeStruct((groups, rows, w), dt) for _, w, dt in _PROJ_OUT]
                   + [jax.ShapeDtypeStruct((groups, r, rows), BF16) for _, r in _PROJ_OUT_T]),
        compiler_params=_cparams(2),
        name="proj_rope" if rope else "proj_ctx",
    )(x, mod3, g2, w_in, wvt, b_gate, qk_gain, ones_bd, cos, sin)
    return dict(zip([n for n, _, _ in _PROJ_OUT] + [n for n, _ in _PROJ_OUT_T], outs))


def _log_sigmoid(x):
    return jnp.minimum(x, 0.0) - jnp.log(1.0 + jnp.exp(-jnp.abs(x)))


def _split_heads(col):
    lane = lax.broadcasted_iota(jnp.int32, col.shape, 1)
    zero = jnp.zeros_like(col)
    return jnp.concatenate([jnp.where(lane < HEAD_DIM, col, zero),
                            jnp.where(lane >= HEAD_DIM, col, zero)], axis=0)


def _ret_kernel(*refs, n_chunks, with_ctx_out):
    if with_ctx_out:
        (q_ref, k_ref, v_ref, g_ref, kc_ref, vc_ref, qc_ref, gc_ref, logit_ref, gain_ref,
         o_ref, oc_ref, mask_ref, dec_ref) = refs
    else:
        q_ref, k_ref, v_ref, g_ref, kc_ref, vc_ref, logit_ref, gain_ref, o_ref, mask_ref, dec_ref = refs
    C = RET_CHUNK
    gain = gain_ref[...]
    log_decay = lambda d, h: _log_sigmoid(logit_ref[d:d + 1, h:h + 1])

    @pl.when(pl.program_id(0) == 0)
    def _():
        di = lax.broadcasted_iota(jnp.int32, (C, C), 0) - lax.broadcasted_iota(jnp.int32, (C, C), 1)
        dpos = jnp.maximum(di, 0).astype(F32)
        dneg = jnp.maximum(-di, 0).astype(F32)
        pos = lax.broadcasted_iota(jnp.int32, (C, 1), 0).astype(F32)
        for h in range(RET_HEADS):
            lf, lb = log_decay(0, h), log_decay(1, h)
            mask_ref[h] = (jnp.where(di >= 0, jnp.exp(lf * dpos), 0.0)
                           + jnp.where(di <= 0, jnp.exp(lb * dneg), 0.0))
            dec_ref[4 * h + 0] = jnp.exp(lf * (pos + 1.0))
            dec_ref[4 * h + 1] = jnp.exp(lb * (C - pos))
            dec_ref[4 * h + 2] = jnp.exp(lf * (C - 1.0 - pos))
            dec_ref[4 * h + 3] = jnp.exp(lb * pos)

    tabs = [dict(mask=mask_ref[h], q_f=dec_ref[4 * h + 0], q_b=dec_ref[4 * h + 1],
                 k_f=dec_ref[4 * h + 2], k_b=dec_ref[4 * h + 3],
                 s_f=jnp.exp(log_decay(0, h) * C), s_b=jnp.exp(log_decay(1, h) * C))
            for h in range(RET_HEADS)]

    def chunk_states(kcol, vfull, h):
        v = vfull[:, h * RET_DV:(h + 1) * RET_DV]
        t = tabs[h]
        return (_dot_tn((kcol * t['k_f']).astype(BF16), v), _dot_tn((kcol * t['k_b']).astype(BF16), v))

    def finish(o, gate):
        return (_rms(o, gain) * gate.astype(F32)).astype(BF16)

    kc_all, vc_all = kc_ref[0], vc_ref[0]
    init = [chunk_states(kc_all[:, (h // 2) * LANES:(h // 2 + 1) * LANES], vc_all, h) for h in range(RET_HEADS)]

    sums = []
    for c in range(n_chunks):
        kc_ = k_ref[0, c * C:(c + 1) * C, :]
        vc_ = v_ref[0, c * C:(c + 1) * C, :]
        sums.append([chunk_states(kc_[:, (h // 2) * LANES:(h // 2 + 1) * LANES], vc_, h) for h in range(RET_HEADS)])
    fwd = [[None] * RET_HEADS for _ in range(n_chunks)]
    bwd = [[None] * RET_HEADS for _ in range(n_chunks)]
    for h in range(RET_HEADS):
        s = init[h][0]
        for c in range(n_chunks):
            fwd[c][h] = s
            s = s * tabs[h]['s_f'] + sums[c][h][0]
        s = init[h][1]
        for c in reversed(range(n_chunks)):
            bwd[c][h] = s
            s = s * tabs[h]['s_b'] + sums[c][h][1]

    def chunk_out(q, k, v, states):
        outs = []
        for j in range(RET_HEADS // 2):
            qs = _split_heads(q[:, j * LANES:(j + 1) * LANES])
            s = _dot_nt(qs.astype(BF16), k[:, j * LANES:(j + 1) * LANES].astype(BF16))
            for hh in range(2):
                h = 2 * j + hh
                t = tabs[h]
                att = (s[hh * C:(hh + 1) * C] * t['mask']).astype(BF16)
                o = _dot(att, v[:, h * RET_DV:(h + 1) * RET_DV])
                if states is not None:
                    qm = qs[hh * C:(hh + 1) * C]
                    sf, sb = states[h]
                    o = o + _dot((qm * t['q_f']).astype(BF16), sf.astype(BF16))
                    o = o + _dot((qm * t['q_b']).astype(BF16), sb.astype(BF16))
                outs.append(o)
        return outs

    for c in range(n_chunks):
        rows = slice(c * C, (c + 1) * C)
        outs = chunk_out(q_ref[0, rows, :], k_ref[0, rows, :], v_ref[0, rows, :],
                         [(fwd[c][h], bwd[c][h]) for h in range(RET_HEADS)])
        for h in range(RET_HEADS):
            cols = slice(h * RET_DV, (h + 1) * RET_DV)
            o_ref[0, rows, cols] = finish(outs[h], g_ref[0, rows, cols])

    if with_ctx_out:
        outs = chunk_out(qc_ref[0], kc_all, vc_all, None)
        for h in range(RET_HEADS):
            cols = slice(h * RET_DV, (h + 1) * RET_DV)
            oc_ref[0, :, cols] = finish(outs[h], gc_ref[0, :, cols])


def _ret_call(pl_, pc_, logit, gain, with_ctx_out):
    b, n, _ = pl_['qb'].shape
    lc = pc_['kb'].shape[1]
    assert lc == RET_CHUNK and n % RET_CHUNK == 0
    full = lambda a: pl.BlockSpec((1,) + a.shape[1:], lambda i: (i, 0, 0))
    ins = [pl_['qb'], pl_['kb'], pl_['vb'], pl_['gb'], pc_['kb'], pc_['vb']]
    if with_ctx_out:
        ins += [pc_['qb'], pc_['gb']]
    in_specs = [full(a) for a in ins] + [_const_spec(logit.shape), _const_spec(gain.shape)]
    out_shape = [jax.ShapeDtypeStruct((b, n, RET_HEADS * RET_DV), BF16)]
    if with_ctx_out:
        out_shape.append(jax.ShapeDtypeStruct((b, lc, RET_HEADS * RET_DV), BF16))
    outs = pl.pallas_call(
        functools.partial(_ret_kernel, n_chunks=n // RET_CHUNK, with_ctx_out=with_ctx_out),
        grid=(b,),
        in_specs=in_specs,
        out_specs=[pl.BlockSpec((1,) + s.shape[1:], lambda i: (i, 0, 0)) for s in out_shape],
        out_shape=out_shape,
        scratch_shapes=[pltpu.VMEM((RET_HEADS, RET_CHUNK, RET_CHUNK), F32),
                        pltpu.VMEM((RET_HEADS * 4, RET_CHUNK, 1), F32)],
        compiler_params=_cparams(1),
        name="retention_ctx" if with_ctx_out else "retention",
    )(*ins, logit, gain)
    return outs if with_ctx_out else (outs[0], None)


def _fold_rows(x, op):
    r, n = x.shape
    return op(x.reshape(r // 8, 8, n), axis=0)


def _attn_kernel(*refs, has_lat, lam_init):
    if has_lat:
        (qa_ref, qc_ref, ka_ref, kc_ref, vat_ref, vct_ref, kac_ref, kcc_ref, vact_ref, vcct_ref,
         dl_ref, dg_ref, oa_ref, oc_ref, s_ref) = refs
    else:
        qa_ref, qc_ref, kac_ref, kcc_ref, vact_ref, vcct_ref, dl_ref, dg_ref, oa_ref, oc_ref, s_ref = refs

    dl = dl_ref[...]
    lam = (jnp.exp(jnp.sum(dl[0:1] * dl[1:2], axis=-1, keepdims=True))
           - jnp.exp(jnp.sum(dl[2:3] * dl[3:4], axis=-1, keepdims=True)) + lam_init)
    dgain = dg_ref[...] * (1.0 - lam_init)
    tq = min(ATTN_Q_SUB, qa_ref.shape[1])
    n_cols = DA_HEADS + GQA_HEADS // 2
    units = [(t, c) for t in range(qa_ref.shape[1] // tq) for c in range(n_cols)]
    t_lat = ka_ref.shape[1] if has_lat else 0
    t_ctx = kac_ref.shape[1]
    kb = ATTN_KEY_BLOCK
    assert t_lat % kb == 0 and t_ctx % kb == 0
    blocks = [(False, r) for r in range(0, t_lat, kb)] + [(True, r) for r in range(0, t_ctx, kb)]

    def column(u):
        t, c = units[u]
        qrows = slice(t * tq, (t + 1) * tq)
        if c < DA_HEADS:
            cols = slice(c * LANES, (c + 1) * LANES)
            return (qa_ref[0, qrows, cols], (ka_ref if has_lat else None, kac_ref), cols,
                    (vat_ref if has_lat else None, vact_ref), slice(c * DA_VROWS, (c + 1) * DA_VROWS), LANES)
        j = c - DA_HEADS
        g = j // 2
        return (qc_ref[0, qrows, slice(j * LANES, (j + 1) * LANES)], (kc_ref if has_lat else None, kcc_ref),
                slice(g * LANES, (g + 1) * LANES),
                (vct_ref if has_lat else None, vcct_ref), slice(g * GQA_VROWS, (g + 1) * GQA_VROWS), HEAD_DIM)

    runtime_zero = jnp.minimum(pl.program_id(1), 0)

    def score_block(u, qs, i):
        _, krefs, kcols, _, _, _ = column(u)
        is_ctx, r = blocks[i]
        s = _dot(krefs[is_ctx][0, r:r + kb, kcols], qs)
        s_ref[u % 2 + runtime_zero, i * kb:(i + 1) * kb, :] = s
        return _fold_rows(s, jnp.max)

    def value_block(u, m, i):
        _, _, _, vrefs, vrows, _ = column(u)
        is_ctx, r = blocks[i]
        e = jnp.exp2(s_ref[u % 2 + runtime_zero, i * kb:(i + 1) * kb, :] - m)
        return _dot(vrefs[is_ctx][0, vrows, r:r + kb], e.astype(BF16))

    def finish(u, o_ext):
        t, c = units[u]
        qrows = slice(t * tq, (t + 1) * tq)
        dv = column(u)[5]
        o = o_ext[:dv] * (1.0 / o_ext[dv:dv + 1])
        o1, o2 = o[:, :tq], o[:, tq:]
        if c < DA_HEADS:
            d = o1 - lam * o2
            d = d * lax.rsqrt(jnp.mean(d * d, axis=0, keepdims=True) + EPS)
            oa_ref[0, qrows, c * LANES:(c + 1) * LANES] = (d.T * dgain).astype(BF16)
        else:
            j = c - DA_HEADS
            oc_ref[0, qrows, j * LANES:(j + 1) * LANES] = jnp.concatenate([o1, o2], axis=0).T.astype(BF16)

    def tree(op, xs):
        return functools.reduce(op, xs)

    def split_heads_t(qcol):
        qt = qcol.astype(F32).T
        row = lax.broadcasted_iota(jnp.int32, qt.shape, 0)
        return jnp.concatenate([jnp.where(row < HEAD_DIM, qt, 0.0), jnp.where(row >= HEAD_DIM, qt, 0.0)],
                               axis=1).astype(BF16)

    qs = split_heads_t(column(0)[0])
    m = jnp.max(tree(jnp.maximum, [score_block(0, qs, i) for i in range(len(blocks))]), axis=0, keepdims=True)
    for u in range(len(units)):
        nxt = u + 1 < len(units)
        if nxt:
            qs_next = split_heads_t(column(u + 1)[0])
        maxes, outs = [], []
        for i in range(len(blocks)):
            if nxt:
                maxes.append(score_block(u + 1, qs_next, i))
            outs.append(value_block(u, m, i))
        finish(u, tree(lambda a, b: a + b, outs))
        if nxt:
            m = jnp.max(tree(jnp.maximum, maxes), axis=0, keepdims=True)


def _attn_call(pl_, pc_, batch, ctx_len, diff_lambda, diff_gain, lam_init, tq):
    has_lat = pl_ is not None
    ctx_tok = lambda w: pl.BlockSpec((1, ctx_len, w), lambda i, j: (0, i, 0))
    ctx_t = lambda r: pl.BlockSpec((1, r, ctx_len), lambda i, j: (0, 0, i))
    if has_lat:
        n = pl_['qa'].shape[1]
        qspec = lambda w: pl.BlockSpec((1, tq, w), lambda i, j: (i, j, 0))
        whole = lambda a: pl.BlockSpec((1,) + a.shape[1:], lambda i, j: (i, 0, 0))
        kvl = [pl_['ka'], pl_['kc'], pl_['vat'], pl_['vct']]
        ins = [pl_['qa'], pl_['qc']] + kvl
        in_specs = [qspec(512), qspec(512)] + [whole(a) for a in kvl]
        grid = (batch, n // tq)
        out_rows, t_total = (batch, n), n + ctx_len
    else:
        assert tq == ctx_len
        qspec = ctx_tok
        ins = [pc_['qa'], pc_['qc']]
        in_specs = [qspec(512), qspec(512)]
        grid = (batch, 1)
        out_rows, t_total = (1, batch * ctx_len), ctx_len
    ins += [pc_['ka'], pc_['kc'], pc_['vat'], pc_['vct'], diff_lambda, diff_gain]
    in_specs += [ctx_tok(512), ctx_tok(256), ctx_t(DA_HEADS * DA_VROWS), ctx_t(GQA_KV * GQA_VROWS),
                 _const_spec(diff_lambda.shape), _const_spec(diff_gain.shape)]
    return pl.pallas_call(
        functools.partial(_attn_kernel, has_lat=has_lat, lam_init=lam_init),
        grid=grid,
        in_specs=in_specs,
        out_specs=[qspec(BRANCH_W), qspec(BRANCH_W)],
        out_shape=[jax.ShapeDtypeStruct(out_rows + (BRANCH_W,), BF16)] * 2,
        scratch_shapes=[pltpu.VMEM((2, t_total, 2 * min(ATTN_Q_SUB, tq)), F32)],
        compiler_params=_cparams(2),
        name="attn_lat" if has_lat else "attn_ctx",
    )(*ins)


def _merge_kernel(h_ref, mod_ref, g_ref, oa_ref, ob_ref, oc_ref, gt_ref, wb_ref, wo_ref, o_ref):
    for rows in _row_blocks(h_ref.shape[1], TOKEN_SUB):
        merged = None
        for i, br in enumerate((oa_ref, ob_ref, oc_ref)):
            p = _dot(br[0, rows, :], wb_ref[i]) * gt_ref[0, rows, i * D_MODEL:(i + 1) * D_MODEL].astype(F32)
            merged = p if merged is None else merged + p
        out = _dot(merged.astype(BF16), wo_ref[...])
        o_ref[0, rows, :] = h_ref[0, rows, :] + mod_ref[0, 2:3, :] * _rms(out, g_ref[1:2, :])


def _merge_call(h, mod3, g2, oa, ob, oc, gt, w_branch, w_out, layer, tm):
    groups, rows, d = h.shape
    tok = lambda w: pl.BlockSpec((1, tm, w), lambda g, i: (g, i, 0))
    return pl.pallas_call(
        _merge_kernel,
        grid=(groups, rows // tm),
        in_specs=[tok(d), pl.BlockSpec((1, 3, d), lambda g, i: (g, 0, 0)), _const_spec(g2.shape),
                  tok(BRANCH_W), tok(BRANCH_W), tok(BRANCH_W), tok(N_BRANCH * d),
                  _layer_spec(w_branch, (layer,)), _layer_spec(w_out, (layer,))],
        out_specs=tok(d),
        out_shape=jax.ShapeDtypeStruct(h.shape, F32),
        compiler_params=_cparams(2),
        name="merge",
    )(h, mod3, g2, oa, ob, oc, gt, w_branch, w_out)


def _rope_tables(n_tok):
    t = np.arange(n_tok)
    n_freq = ROPE_DIM // 4
    inv = ROPE_THETA ** (-np.arange(n_freq, dtype=np.float64) / n_freq)
    ang = np.concatenate([(t // GRID_W)[:, None] * inv, (t % GRID_W)[:, None] * inv], axis=-1)
    cos, sin = np.cos(ang), np.sin(ang)
    cos128 = np.tile(cos, (1, LANES // (ROPE_DIM // 2)))
    sin128 = np.tile(np.concatenate([-sin, sin], axis=-1), (1, LANES // ROPE_DIM))
    return jnp.asarray(cos128, F32), jnp.asarray(sin128, F32)


def _ones_block_diag():
    i = np.arange(LANES)
    return jnp.asarray((i[:, None] // HEAD_DIM) == (i[None, :] // HEAD_DIM), BF16)


def kernel(x, c, ctx, c_ctx, w_mod, b_mod, norm_g, w_ffn_in, w_ffn_out, w_in, b_gate, diff_lambda,
           diff_norm_g, ret_decay_logit, ret_norm_g, qk_norm_g, w_branch, w_out):
    batch, seq, d = x.shape
    ctx_len = ctx.shape[1]
    depth = w_mod.shape[0]
    assert d == D_MODEL and seq % TOKEN_TILE == 0 and (batch * ctx_len) % TOKEN_TILE == 0

    pad = (-(batch + 1)) % 8
    c_all = jnp.concatenate([c, c_ctx[None, :], jnp.zeros((pad, d), F32)], axis=0)
    mod_all = _mod_call(c_all, w_mod, b_mod).reshape(depth, batch + 1 + pad, N_SUB, 3, d)

    wi = w_ffn_in.astype(BF16)
    wo = w_ffn_out.astype(BF16)
    w_in_b = w_in.astype(BF16)
    wb = w_branch.astype(BF16)
    wout = w_out.astype(BF16)
    wvt = _wvt_call(w_in)
    bg = b_gate[:, None, :]
    cos, sin = _rope_tables(seq)
    ones_bd = _ones_block_diag()
    qk_gain = jnp.tile(qk_norm_g, (1, 1, LANES // HEAD_DIM))

    h = x
    hc = ctx.reshape(1, batch * ctx_len, d)
    for l in range(depth):
        last = l == depth - 1
        lam_init = 0.8 - 0.6 * math.exp(-0.3 * l)
        mod_l = lambda s: mod_all[l, :batch, s]
        mod_c = lambda s: mod_all[l, batch:batch + 1, s]
        ng = norm_g[l]

        h = _ffn_call(h, mod_l(0), ng[0:2], wi, wo, (l, 0), TOKEN_TILE)
        hc = _ffn_call(hc, mod_c(0), ng[0:2], wi, wo, (l, 0), TOKEN_TILE)

        p_lat = _proj_call(h, mod_l(1), ng[2:4], w_in_b, wvt, bg, l, qk_gain[l], ones_bd, cos, sin,
                           PROJ_TILE, True)
        p_ctx = _proj_call(hc, mod_c(1), ng[2:4], w_in_b, wvt, bg, l, qk_gain[l], ones_bd, cos, sin,
                           PROJ_TILE, False)
        per_batch = lambda a: a.reshape(batch, ctx_len, a.shape[-1])
        p_ctx_b = {k: per_batch(p_ctx[k]) for k in ('qb', 'kb', 'vb', 'gb')}

        ob, ob_c = _ret_call(p_lat, p_ctx_b, ret_decay_logit[l], ret_norm_g[l][None, :], not last)
        dgain = diff_norm_g[l][None, :]
        oa, oc = _attn_call(p_lat, p_ctx, batch, ctx_len, diff_lambda[l], dgain, lam_init, ATTN_Q_TILE)
        h = _merge_call(h, mod_l(1), ng[2:4], oa, ob, oc, p_lat['gt'], wb, wout, l, TOKEN_TILE)
        if not last:
            oa_c, oc_c = _attn_call(None, p_ctx, batch, ctx_len, diff_lambda[l], dgain, lam_init, ctx_len)
            hc = _merge_call(hc, mod_c(1), ng[2:4], oa_c, ob_c.reshape(1, batch * ctx_len, BRANCH_W), oc_c,
                             p_ctx['gt'], wb, wout, l, TOKEN_TILE)

        h = _ffn_call(h, mod_l(2), ng[4:6], wi, wo, (l, 1), TOKEN_TILE)
        if not last:
            hc = _ffn_call(hc, mod_c(2), ng[4:6], wi, wo, (l, 1), TOKEN_TILE)
    return h
```

```python
import functools
import math

import numpy as np
import jax
import jax.numpy as jnp
from jax import lax
from jax.experimental import pallas as pl
from jax.experimental.pallas import tpu as pltpu

F32 = jnp.float32
BF16 = jnp.bfloat16

D_MODEL = 1024
N_SUB = 3
D_FF = 2816
EPS = 1e-6
ROPE_THETA = 10000.0
ROPE_DIM = 64
GRID_W = 64
HEAD_DIM = 64
LANES = 128
DA_HEADS = 4
RET_HEADS = 4
RET_DV = 128
GQA_HEADS = 8
GQA_KV = 2
N_BRANCH = 3
BRANCH_W = 512
QK_SCALE = HEAD_DIM ** -0.5
SOFTMAX_Q_SCALE = QK_SCALE * math.log2(math.e)
MXU_TILE = 256
RET_CHUNK = MXU_TILE
ATTN_KEY_BLOCK = MXU_TILE
TOKEN_SUB = 2 * MXU_TILE
TOKEN_TILE = 2 * TOKEN_SUB
PROJ_TILE = 2 * MXU_TILE
ATTN_Q_SUB = MXU_TILE
ATTN_Q_TILE = 4 * ATTN_Q_SUB
PROJ_ROW_SUB = MXU_TILE
ONES_ROWS = 16
DA_VROWS = LANES + ONES_ROWS
GQA_VROWS = HEAD_DIM + ONES_ROWS
VMEM_LIMIT = 56 * 1024 * 1024

_SIZES = (512, 512, 512, 256, 256, 512, 512, 512, 128, 128, N_BRANCH * D_MODEL)
_NAMES = ('a_q', 'a_k', 'a_v', 'b_q', 'b_k', 'b_v', 'b_g', 'c_q', 'c_k', 'c_v', 'gate')
_SPAN = {}
_lo = 0
for _n, _s in zip(_NAMES, _SIZES):
    _SPAN[_n] = (_lo, _lo + _s)
    _lo += _s
IN_COLS = _lo


def _cparams(n_axes):
    return pltpu.CompilerParams(dimension_semantics=("arbitrary",) * n_axes,
                                vmem_limit_bytes=VMEM_LIMIT)


def _const_spec(shape):
    nd = len(shape)
    return pl.BlockSpec(shape, lambda *_: (0,) * nd, pipeline_mode=pl.Buffered(1))


def _rms(x, g):
    return x * lax.rsqrt(jnp.mean(x * x, axis=-1, keepdims=True) + EPS) * g


def _sigmoid(x):
    return 1.0 / (1.0 + jnp.exp(-x))


def _dot(a, b):
    return jnp.dot(a, b, preferred_element_type=F32)


def _dot_nt(a, b):
    return lax.dot_general(a, b, (((1,), (1,)), ((), ())), preferred_element_type=F32)


def _dot_tn(a, b):
    return lax.dot_general(a, b, (((0,), (0,)), ((), ())), preferred_element_type=F32)


def _mod_kernel(c_ref, w_ref, b_ref, o_ref):
    c = c_ref[...]
    x = (c * _sigmoid(c)).astype(BF16)
    o_ref[0] = _dot(x, w_ref[0].astype(BF16)) + b_ref[0]


def _mod_call(c_all, w_mod, b_mod):
    depth, d, cols = w_mod.shape
    rows = c_all.shape[0]
    tn = 1536
    return pl.pallas_call(
        _mod_kernel,
        grid=(depth, cols // tn),
        in_specs=[pl.BlockSpec((rows, d), lambda l, j: (0, 0)),
                  pl.BlockSpec((1, d, tn), lambda l, j: (l, 0, j)),
                  pl.BlockSpec((1, 1, tn), lambda l, j: (l, 0, j))],
        out_specs=pl.BlockSpec((1, rows, tn), lambda l, j: (l, 0, j)),
        out_shape=jax.ShapeDtypeStruct((depth, rows, cols), F32),
        compiler_params=_cparams(2),
        name="mod",
    )(c_all, w_mod, b_mod.reshape(depth, 1, cols))


def _wvt_kernel(a_ref, c_ref, o_ref):
    n_a = a_ref.shape[1]
    o_ref[0:n_a, :] = a_ref[...].T.astype(BF16)
    o_ref[n_a:, :] = c_ref[...].T.astype(BF16)


def _wvt_call(w_in):
    depth, d, _ = w_in.shape
    (a_lo, a_hi), (c_lo, c_hi) = _SPAN['a_v'], _SPAN['c_v']
    n_a, n_c = a_hi - a_lo, c_hi - c_lo
    assert a_lo % n_a == 0 and c_lo % n_c == 0
    return pl.pallas_call(
        _wvt_kernel,
        grid=(depth,),
        in_specs=[pl.BlockSpec((None, d, n_a), lambda l: (l, 0, a_lo // n_a)),
                  pl.BlockSpec((None, d, n_c), lambda l: (l, 0, c_lo // n_c))],
        out_specs=pl.BlockSpec((None, n_a + n_c, d), lambda l: (l, 0, 0)),
        out_shape=jax.ShapeDtypeStruct((depth, n_a + n_c, d), BF16),
        compiler_params=_cparams(1),
        name="wvt",
    )(w_in, w_in)


def _row_blocks(tm, sub):
    return [slice(r, r + sub) for r in range(0, tm, sub)]


def _ffn_kernel(x_ref, mod_ref, g_ref, wi_ref, wo_ref, o_ref):
    shift, scale, gate = mod_ref[0, 0:1, :], mod_ref[0, 1:2, :], mod_ref[0, 2:3, :]
    blocks = _row_blocks(x_ref.shape[1], TOKEN_SUB)
    ys = [(_rms(x_ref[0, rows, :], g_ref[0:1, :]) * (1.0 + scale) + shift).astype(BF16) for rows in blocks]
    for rows, y in zip(blocks, ys):
        x = x_ref[0, rows, :]
        acc = None
        for c in range(D_FF // MXU_TILE):
            lo = c * MXU_TILE
            a = _dot(y, wi_ref[:, lo:lo + MXU_TILE])
            u = _dot(y, wi_ref[:, D_FF + lo:D_FF + lo + MXU_TILE])
            hid = (a * _sigmoid(a) * u).astype(BF16)
            part = _dot(hid, wo_ref[lo:lo + MXU_TILE, :])
            acc = part if acc is None else acc + part
        o_ref[0, rows, :] = x + 0.5 * gate * _rms(acc, g_ref[1:2, :])


def _layer_spec(stacked, lead):
    shape = stacked.shape[len(lead):]
    nd = len(shape)
    return pl.BlockSpec((None,) * len(lead) + shape, lambda *_: tuple(lead) + (0,) * nd,
                        pipeline_mode=pl.Buffered(1))


def _ffn_call(x, mod3, g2, wi, wo, lead, tm):
    groups, rows, d = x.shape
    return pl.pallas_call(
        _ffn_kernel,
        grid=(groups, rows // tm),
        in_specs=[pl.BlockSpec((1, tm, d), lambda g, i: (g, i, 0)),
                  pl.BlockSpec((1, 3, d), lambda g, i: (g, 0, 0)),
                  _const_spec(g2.shape), _layer_spec(wi, lead), _layer_spec(wo, lead)],
        out_specs=pl.BlockSpec((1, tm, d), lambda g, i: (g, i, 0)),
        out_shape=jax.ShapeDtypeStruct(x.shape, F32),
        compiler_params=_cparams(2),
        name="ffn",
    )(x, mod3, g2, wi, wo)


def _swap_half_heads(x):
    lane = lax.broadcasted_iota(jnp.int32, x.shape, 1)
    return jnp.where((lane & (HEAD_DIM - 1)) < HEAD_DIM // 2,
                     pltpu.roll(x, LANES - HEAD_DIM // 2, axis=1),
                     pltpu.roll(x, HEAD_DIM // 2, axis=1))


def _rope_cols(p, cos, sin):
    cols = []
    for j in range(p.shape[1] // LANES):
        x = p[:, j * LANES:(j + 1) * LANES]
        cols.append(x * cos + _swap_half_heads(x) * sin)
    return jnp.concatenate(cols, axis=1) if len(cols) > 1 else cols[0]


def _head_rms_cols(p, gain, ones_bd):
    cols = []
    for j in range(p.shape[1] // LANES):
        x = p[:, j * LANES:(j + 1) * LANES]
        ms = _dot((x * x).astype(BF16), ones_bd) * (1.0 / HEAD_DIM)
        cols.append(x * lax.rsqrt(ms + EPS) * gain)
    return jnp.concatenate(cols, axis=1) if len(cols) > 1 else cols[0]


def _dup_heads(x):
    lane = lax.broadcasted_iota(jnp.int32, x.shape, 1)
    r = pltpu.roll(x, HEAD_DIM, axis=1)
    low = lane < HEAD_DIM
    return jnp.concatenate([jnp.where(low, x, r), jnp.where(low, r, x)], axis=1)


def _proj_kernel(x_ref, mod_ref, g_ref, w_ref, wvt_ref, bg_ref, qkg_ref, bd_ref, cos_ref, sin_ref,
                 *out_refs, rope, names):
    out = dict(zip(names, out_refs))
    shift, scale = mod_ref[0, 0:1, :], mod_ref[0, 1:2, :]
    bd = bd_ref[...]
    ones = jnp.ones((ONES_ROWS, PROJ_ROW_SUB), BF16)
    blocks = _row_blocks(x_ref.shape[1], PROJ_ROW_SUB)
    ys = [(_rms(x_ref[0, rows, :], g_ref[0:1, :]) * (1.0 + scale) + shift).astype(BF16) for rows in blocks]
    for rows, y in zip(blocks, ys):
        cos, sin = cos_ref[rows, :], sin_ref[rows, :]

        def proj(name):
            lo, hi = _SPAN[name]
            return _dot(y, w_ref[:, lo:hi])

        def pos(p):
            return _rope_cols(p, cos, sin) if rope else p

        if 'qc' in out:
            cq = proj('c_q')
        ck = proj('c_k')
        if 'qa' in out:
            out['qa'][0, rows, :] = (pos(proj('a_q')) * SOFTMAX_Q_SCALE).astype(BF16)
        out['ka'][0, rows, :] = pos(proj('a_k')).astype(BF16)
        if 'qc' in out:
            out['qc'][0, rows, :] = (pos(_head_rms_cols(cq, qkg_ref[0:1, :], bd))
                                     * SOFTMAX_Q_SCALE).astype(BF16)
        out['kc'][0, rows, :] = _dup_heads(pos(_head_rms_cols(ck, qkg_ref[1:2, :], bd))).astype(BF16)
        if 'qb' in out:
            out['qb'][0, rows, :] = pos(proj('b_q')) * QK_SCALE
        out['kb'][0, rows, :] = pos(proj('b_k'))
        out['vb'][0, rows, :] = proj('b_v').astype(BF16)
        if 'gb' in out:
            g = proj('b_g')
            out['gb'][0, rows, :] = (g * _sigmoid(g)).astype(BF16)
        if 'gt' in out:
            glo = _SPAN['gate'][0]
            for i in range(N_BRANCH):
                lo = glo + i * D_MODEL
                pre = _dot(y, w_ref[:, lo:lo + D_MODEL]) + bg_ref[:, i * D_MODEL:(i + 1) * D_MODEL]
                out['gt'][0, rows, i * D_MODEL:(i + 1) * D_MODEL] = _sigmoid(pre).astype(BF16)
        vt = _dot_nt(wvt_ref[...], y).astype(BF16)
        vat_ref, vct_ref = out['vat'], out['vct']
        for c in range(DA_HEADS):
            vat_ref[0, c * DA_VROWS:c * DA_VROWS + LANES, rows] = vt[c * LANES:(c + 1) * LANES]
            vat_ref[0, c * DA_VROWS + LANES:(c + 1) * DA_VROWS, rows] = ones
        for g_ in range(GQA_KV):
            src = DA_HEADS * LANES + g_ * HEAD_DIM
            vct_ref[0, g_ * GQA_VROWS:g_ * GQA_VROWS + HEAD_DIM, rows] = vt[src:src + HEAD_DIM]
            vct_ref[0, g_ * GQA_VROWS + HEAD_DIM:(g_ + 1) * GQA_VROWS, rows] = ones


_PROJ_OUT = (('qa', 512, BF16), ('ka', 512, BF16), ('qb', 256, F32), ('kb', 256, F32),
             ('vb', 512, BF16), ('gb', 512, BF16), ('qc', 512, BF16), ('kc', 256, BF16),
             ('gt', N_BRANCH * D_MODEL, BF16))
_PROJ_OUT_T = (('vat', DA_HEADS * DA_VROWS), ('vct', GQA_KV * GQA_VROWS))
_PROJ_KV_ONLY = ('ka', 'kb', 'vb', 'kc', 'vat', 'vct')


def _proj_call(x, mod3, g2, w_in, wvt, b_gate, layer, qk_gain, ones_bd, cos, sin, tm, rope, kv_only=False):
    groups, rows, d = x.shape
    tok = lambda w: pl.BlockSpec((1, tm, w), lambda g, i: (g, i, 0))
    tok_t = lambda r: pl.BlockSpec((1, r, tm), lambda g, i: (g, 0, i))
    pos_map = (lambda g, i: (i, 0)) if rope else (lambda g, i: (0, 0))
    keep = lambda n: not kv_only or n in _PROJ_KV_ONLY
    out_tok = [(n, w, dt) for n, w, dt in _PROJ_OUT if keep(n)]
    out_t = [(n, r) for n, r in _PROJ_OUT_T if keep(n)]
    names = tuple([n for n, _, _ in out_tok] + [n for n, _ in out_t])
    outs = pl.pallas_call(
        functools.partial(_proj_kernel, rope=rope, names=names),
        grid=(groups, rows // tm),
        in_specs=[tok(d),
                  pl.BlockSpec((1, 3, d), lambda g, i: (g, 0, 0)),
                  _const_spec(g2.shape), _layer_spec(w_in, (layer,)), _layer_spec(wvt, (layer,)),
                  _layer_spec(b_gate, (layer,)),
                  _const_spec(qk_gain.shape), _const_spec(ones_bd.shape),
                  pl.BlockSpec((tm, LANES), pos_map), pl.BlockSpec((tm, LANES), pos_map)],
        out_specs=[tok(w) for _, w, _ in out_tok] + [tok_t(r) for _, r in out_t],
        out_shape=([jax.ShapeDtypeStruct((groups, rows, w), dt) for _, w, dt in out_tok]
                   + [jax.ShapeDtypeStruct((groups, r, rows), BF16) for _, r in out_t]),
        compiler_params=_cparams(2),
        name="proj_rope" if rope else ("proj_ctx_kv" if kv_only else "proj_ctx"),
    )(x, mod3, g2, w_in, wvt, b_gate, qk_gain, ones_bd, cos, sin)
    return dict(zip(names, outs))


def _log_sigmoid(x):
    return jnp.minimum(x, 0.0) - jnp.log(1.0 + jnp.exp(-jnp.abs(x)))


def _split_heads(col):
    lane = lax.broadcasted_iota(jnp.int32, col.shape, 1)
    zero = jnp.zeros_like(col)
    return jnp.concatenate([jnp.where(lane < HEAD_DIM, col, zero),
                            jnp.where(lane >= HEAD_DIM, col, zero)], axis=0)


def _ret_kernel(*refs, n_chunks, with_ctx_out):
    if with_ctx_out:
        (q_ref, k_ref, v_ref, g_ref, kc_ref, vc_ref, qc_ref, gc_ref, logit_ref, gain_ref,
         o_ref, oc_ref, mask_ref, dec_ref) = refs
    else:
        q_ref, k_ref, v_ref, g_ref, kc_ref, vc_ref, logit_ref, gain_ref, o_ref, mask_ref, dec_ref = refs
    C = RET_CHUNK
    gain = gain_ref[...]
    log_decay = lambda d, h: _log_sigmoid(logit_ref[d:d + 1, h:h + 1])

    @pl.when(pl.program_id(0) == 0)
    def _():
        di = lax.broadcasted_iota(jnp.int32, (C, C), 0) - lax.broadcasted_iota(jnp.int32, (C, C), 1)
        dpos = jnp.maximum(di, 0).astype(F32)
        dneg = jnp.maximum(-di, 0).astype(F32)
        pos = lax.broadcasted_iota(jnp.int32, (C, 1), 0).astype(F32)
        for h in range(RET_HEADS):
            lf, lb = log_decay(0, h), log_decay(1, h)
            mask_ref[h] = (jnp.where(di >= 0, jnp.exp(lf * dpos), 0.0)
                           + jnp.where(di <= 0, jnp.exp(lb * dneg), 0.0))
            dec_ref[4 * h + 0] = jnp.exp(lf * (pos + 1.0))
            dec_ref[4 * h + 1] = jnp.exp(lb * (C - pos))
            dec_ref[4 * h + 2] = jnp.exp(lf * (C - 1.0 - pos))
            dec_ref[4 * h + 3] = jnp.exp(lb * pos)

    tabs = [dict(mask=mask_ref[h], q_f=dec_ref[4 * h + 0], q_b=dec_ref[4 * h + 1],
                 k_f=dec_ref[4 * h + 2], k_b=dec_ref[4 * h + 3],
                 s_f=jnp.exp(log_decay(0, h) * C), s_b=jnp.exp(log_decay(1, h) * C))
            for h in range(RET_HEADS)]

    def chunk_states(kcol, vfull, h):
        v = vfull[:, h * RET_DV:(h + 1) * RET_DV]
        t = tabs[h]
        return (_dot_tn((kcol * t['k_f']).astype(BF16), v), _dot_tn((kcol * t['k_b']).astype(BF16), v))

    def finish(o, gate):
        return (_rms(o, gain) * gate.astype(F32)).astype(BF16)

    kc_all, vc_all = kc_ref[0], vc_ref[0]
    init = [chunk_states(kc_all[:, (h // 2) * LANES:(h // 2 + 1) * LANES], vc_all, h) for h in range(RET_HEADS)]

    sums = []
    for c in range(n_chunks):
        kc_ = k_ref[0, c * C:(c + 1) * C, :]
        vc_ = v_ref[0, c * C:(c + 1) * C, :]
        sums.append([chunk_states(kc_[:, (h // 2) * LANES:(h // 2 + 1) * LANES], vc_, h) for h in range(RET_HEADS)])
    fwd = [[None] * RET_HEADS for _ in range(n_chunks)]
    bwd = [[None] * RET_HEADS for _ in range(n_chunks)]
    for h in range(RET_HEADS):
        s = init[h][0]
        for c in range(n_chunks):
            fwd[c][h] = s
            s = s * tabs[h]['s_f'] + sums[c][h][0]
        s = init[h][1]
        for c in reversed(range(n_chunks)):
            bwd[c][h] = s
            s = s * tabs[h]['s_b'] + sums[c][h][1]

    def chunk_out(q, k, v, states):
        outs = []
        for j in range(RET_HEADS // 2):
            qs = _split_heads(q[:, j * LANES:(j + 1) * LANES])
            s = _dot_nt(qs.astype(BF16), k[:, j * LANES:(j + 1) * LANES].astype(BF16))
            for hh in range(2):
                h = 2 * j + hh
                t = tabs[h]
                att = (s[hh * C:(hh + 1) * C] * t['mask']).astype(BF16)
                o = _dot(att, v[:, h * RET_DV:(h + 1) * RET_DV])
                if states is not None:
                    qm = qs[hh * C:(hh + 1) * C]
                    sf, sb = states[h]
                    o = o + _dot((qm * t['q_f']).astype(BF16), sf.astype(BF16))
                    o = o + _dot((qm * t['q_b']).astype(BF16), sb.astype(BF16))
                outs.append(o)
        return outs

    for c in range(n_chunks):
        rows = slice(c * C, (c + 1) * C)
        outs = chunk_out(q_ref[0, rows, :], k_ref[0, rows, :], v_ref[0, rows, :],
                         [(fwd[c][h], bwd[c][h]) for h in range(RET_HEADS)])
        for h in range(RET_HEADS):
            cols = slice(h * RET_DV, (h + 1) * RET_DV)
            o_ref[0, rows, cols] = finish(outs[h], g_ref[0, rows, cols])

    if with_ctx_out:
        outs = chunk_out(qc_ref[0], kc_all, vc_all, None)
        for h in range(RET_HEADS):
            cols = slice(h * RET_DV, (h + 1) * RET_DV)
            oc_ref[0, :, cols] = finish(outs[h], gc_ref[0, :, cols])


def _ret_call(pl_, pc_, logit, gain, with_ctx_out):
    b, n, _ = pl_['qb'].shape
    lc = pc_['kb'].shape[1]
    assert lc == RET_CHUNK and n % RET_CHUNK == 0
    full = lambda a: pl.BlockSpec((1,) + a.shape[1:], lambda i: (i, 0, 0))
    ins = [pl_['qb'], pl_['kb'], pl_['vb'], pl_['gb'], pc_['kb'], pc_['vb']]
    if with_ctx_out:
        ins += [pc_['qb'], pc_['gb']]
    in_specs = [full(a) for a in ins] + [_const_spec(logit.shape), _const_spec(gain.shape)]
    out_shape = [jax.ShapeDtypeStruct((b, n, RET_HEADS * RET_DV), BF16)]
    if with_ctx_out:
        out_shape.append(jax.ShapeDtypeStruct((b, lc, RET_HEADS * RET_DV), BF16))
    outs = pl.pallas_call(
        functools.partial(_ret_kernel, n_chunks=n // RET_CHUNK, with_ctx_out=with_ctx_out),
        grid=(b,),
        in_specs=in_specs,
        out_specs=[pl.BlockSpec((1,) + s.shape[1:], lambda i: (i, 0, 0)) for s in out_shape],
        out_shape=out_shape,
        scratch_shapes=[pltpu.VMEM((RET_HEADS, RET_CHUNK, RET_CHUNK), F32),
                        pltpu.VMEM((RET_HEADS * 4, RET_CHUNK, 1), F32)],
        compiler_params=_cparams(1),
        name="retention_ctx" if with_ctx_out else "retention",
    )(*ins, logit, gain)
    return outs if with_ctx_out else (outs[0], None)


def _fold_rows(x, op):
    r, n = x.shape
    return op(x.reshape(r // 8, 8, n), axis=0)


def _attn_kernel(*refs, has_lat, lam_init):
    if has_lat:
        (qa_ref, qc_ref, ka_ref, kc_ref, vat_ref, vct_ref, kac_ref, kcc_ref, vact_ref, vcct_ref,
         dl_ref, dg_ref, oa_ref, oc_ref, s_ref) = refs
    else:
        qa_ref, qc_ref, kac_ref, kcc_ref, vact_ref, vcct_ref, dl_ref, dg_ref, oa_ref, oc_ref, s_ref = refs

    dl = dl_ref[...]
    lam = (jnp.exp(jnp.sum(dl[0:1] * dl[1:2], axis=-1, keepdims=True))
           - jnp.exp(jnp.sum(dl[2:3] * dl[3:4], axis=-1, keepdims=True)) + lam_init)
    dgain = dg_ref[...] * (1.0 - lam_init)
    tq = min(ATTN_Q_SUB, qa_ref.shape[1])
    n_cols = DA_HEADS + GQA_HEADS // 2
    units = [(t, c) for t in range(qa_ref.shape[1] // tq) for c in range(n_cols)]
    t_lat = ka_ref.shape[1] if has_lat else 0
    t_ctx = kac_ref.shape[1]
    kb = ATTN_KEY_BLOCK
    assert t_lat % kb == 0 and t_ctx % kb == 0
    blocks = [(False, r) for r in range(0, t_lat, kb)] + [(True, r) for r in range(0, t_ctx, kb)]

    def column(u):
        t, c = units[u]
        qrows = slice(t * tq, (t + 1) * tq)
        if c < DA_HEADS:
            cols = slice(c * LANES, (c + 1) * LANES)
            return (qa_ref[0, qrows, cols], (ka_ref if has_lat else None, kac_ref), cols,
                    (vat_ref if has_lat else None, vact_ref), slice(c * DA_VROWS, (c + 1) * DA_VROWS), LANES)
        j = c - DA_HEADS
        g = j // 2
        return (qc_ref[0, qrows, slice(j * LANES, (j + 1) * LANES)], (kc_ref if has_lat else None, kcc_ref),
                slice(g * LANES, (g + 1) * LANES),
                (vct_ref if has_lat else None, vcct_ref), slice(g * GQA_VROWS, (g + 1) * GQA_VROWS), HEAD_DIM)

    runtime_zero = jnp.minimum(pl.program_id(1), 0)

    def score_block(u, qs, i):
        _, krefs, kcols, _, _, _ = column(u)
        is_ctx, r = blocks[i]
        s = _dot(krefs[is_ctx][0, r:r + kb, kcols], qs)
        s_ref[u % 2 + runtime_zero, i * kb:(i + 1) * kb, :] = s
        return _fold_rows(s, jnp.max)

    def value_block(u, m, i):
        _, _, _, vrefs, vrows, _ = column(u)
        is_ctx, r = blocks[i]
        e = jnp.exp2(s_ref[u % 2 + runtime_zero, i * kb:(i + 1) * kb, :] - m)
        return _dot(vrefs[is_ctx][0, vrows, r:r + kb], e.astype(BF16))

    def finish(u, o_ext):
        t, c = units[u]
        qrows = slice(t * tq, (t + 1) * tq)
        dv = column(u)[5]
        o = o_ext[:dv] * (1.0 / o_ext[dv:dv + 1])
        o1, o2 = o[:, :tq], o[:, tq:]
        if c < DA_HEADS:
            d = o1 - lam * o2
            d = d * lax.rsqrt(jnp.mean(d * d, axis=0, keepdims=True) + EPS)
            oa_ref[0, qrows, c * LANES:(c + 1) * LANES] = (d.T * dgain).astype(BF16)
        else:
            j = c - DA_HEADS
            oc_ref[0, qrows, j * LANES:(j + 1) * LANES] = jnp.concatenate([o1, o2], axis=0).T.astype(BF16)

    def tree(op, xs):
        return functools.reduce(op, xs)

    def split_heads_t(qcol):
        qt = qcol.astype(F32).T
        row = lax.broadcasted_iota(jnp.int32, qt.shape, 0)
        return jnp.concatenate([jnp.where(row < HEAD_DIM, qt, 0.0), jnp.where(row >= HEAD_DIM, qt, 0.0)],
                               axis=1).astype(BF16)

    qs = split_heads_t(column(0)[0])
    m = jnp.max(tree(jnp.maximum, [score_block(0, qs, i) for i in range(len(blocks))]), axis=0, keepdims=True)
    for u in range(len(units)):
        nxt = u + 1 < len(units)
        if nxt:
            qs_next = split_heads_t(column(u + 1)[0])
        maxes, outs = [], []
        for i in range(len(blocks)):
            if nxt:
                maxes.append(score_block(u + 1, qs_next, i))
            outs.append(value_block(u, m, i))
        finish(u, tree(lambda a, b: a + b, outs))
        if nxt:
            m = jnp.max(tree(jnp.maximum, maxes), axis=0, keepdims=True)


def _attn_call(pl_, pc_, batch, ctx_len, diff_lambda, diff_gain, lam_init, tq):
    has_lat = pl_ is not None
    ctx_tok = lambda w: pl.BlockSpec((1, ctx_len, w), lambda i, j: (0, i, 0))
    ctx_t = lambda r: pl.BlockSpec((1, r, ctx_len), lambda i, j: (0, 0, i))
    if has_lat:
        n = pl_['qa'].shape[1]
        qspec = lambda w: pl.BlockSpec((1, tq, w), lambda i, j: (i, j, 0))
        whole = lambda a: pl.BlockSpec((1,) + a.shape[1:], lambda i, j: (i, 0, 0))
        kvl = [pl_['ka'], pl_['kc'], pl_['vat'], pl_['vct']]
        ins = [pl_['qa'], pl_['qc']] + kvl
        in_specs = [qspec(512), qspec(512)] + [whole(a) for a in kvl]
        grid = (batch, n // tq)
        out_rows, t_total = (batch, n), n + ctx_len
    else:
        assert tq == ctx_len
        qspec = ctx_tok
        ins = [pc_['qa'], pc_['qc']]
        in_specs = [qspec(512), qspec(512)]
        grid = (batch, 1)
        out_rows, t_total = (1, batch * ctx_len), ctx_len
    ins += [pc_['ka'], pc_['kc'], pc_['vat'], pc_['vct'], diff_lambda, diff_gain]
    in_specs += [ctx_tok(512), ctx_tok(256), ctx_t(DA_HEADS * DA_VROWS), ctx_t(GQA_KV * GQA_VROWS),
                 _const_spec(diff_lambda.shape), _const_spec(diff_gain.shape)]
    return pl.pallas_call(
        functools.partial(_attn_kernel, has_lat=has_lat, lam_init=lam_init),
        grid=grid,
        in_specs=in_specs,
        out_specs=[qspec(BRANCH_W), qspec(BRANCH_W)],
        out_shape=[jax.ShapeDtypeStruct(out_rows + (BRANCH_W,), BF16)] * 2,
        scratch_shapes=[pltpu.VMEM((2, t_total, 2 * min(ATTN_Q_SUB, tq)), F32)],
        compiler_params=_cparams(2),
        name="attn_lat" if has_lat else "attn_ctx",
    )(*ins)


def _merge_kernel(h_ref, mod_ref, g_ref, oa_ref, ob_ref, oc_ref, gt_ref, wb_ref, wo_ref, o_ref):
    for rows in _row_blocks(h_ref.shape[1], TOKEN_SUB):
        merged = None
        for i, br in enumerate((oa_ref, ob_ref, oc_ref)):
            p = _dot(br[0, rows, :], wb_ref[i]) * gt_ref[0, rows, i * D_MODEL:(i + 1) * D_MODEL].astype(F32)
            merged = p if merged is None else merged + p
        out = _dot(merged.astype(BF16), wo_ref[...])
        o_ref[0, rows, :] = h_ref[0, rows, :] + mod_ref[0, 2:3, :] * _rms(out, g_ref[1:2, :])


def _merge_call(h, mod3, g2, oa, ob, oc, gt, w_branch, w_out, layer, tm):
    groups, rows, d = h.shape
    tok = lambda w: pl.BlockSpec((1, tm, w), lambda g, i: (g, i, 0))
    return pl.pallas_call(
        _merge_kernel,
        grid=(groups, rows // tm),
        in_specs=[tok(d), pl.BlockSpec((1, 3, d), lambda g, i: (g, 0, 0)), _const_spec(g2.shape),
                  tok(BRANCH_W), tok(BRANCH_W), tok(BRANCH_W), tok(N_BRANCH * d),
                  _layer_spec(w_branch, (layer,)), _layer_spec(w_out, (layer,))],
        out_specs=tok(d),
        out_shape=jax.ShapeDtypeStruct(h.shape, F32),
        compiler_params=_cparams(2),
        name="merge",
    )(h, mod3, g2, oa, ob, oc, gt, w_branch, w_out)


def _rope_tables(n_tok):
    t = np.arange(n_tok)
    n_freq = ROPE_DIM // 4
    inv = ROPE_THETA ** (-np.arange(n_freq, dtype=np.float64) / n_freq)
    ang = np.concatenate([(t // GRID_W)[:, None] * inv, (t % GRID_W)[:, None] * inv], axis=-1)
    cos, sin = np.cos(ang), np.sin(ang)
    cos128 = np.tile(cos, (1, LANES // (ROPE_DIM // 2)))
    sin128 = np.tile(np.concatenate([-sin, sin], axis=-1), (1, LANES // ROPE_DIM))
    return jnp.asarray(cos128, F32), jnp.asarray(sin128, F32)


def _ones_block_diag():
    i = np.arange(LANES)
    return jnp.asarray((i[:, None] // HEAD_DIM) == (i[None, :] // HEAD_DIM), BF16)


def kernel(x, c, ctx, c_ctx, w_mod, b_mod, norm_g, w_ffn_in, w_ffn_out, w_in, b_gate, diff_lambda,
           diff_norm_g, ret_decay_logit, ret_norm_g, qk_norm_g, w_branch, w_out):
    batch, seq, d = x.shape
    ctx_len = ctx.shape[1]
    depth = w_mod.shape[0]
    assert d == D_MODEL and seq % TOKEN_TILE == 0 and (batch * ctx_len) % TOKEN_TILE == 0

    pad = (-(batch + 1)) % 8
    c_all = jnp.concatenate([c, c_ctx[None, :], jnp.zeros((pad, d), F32)], axis=0)
    mod_all = _mod_call(c_all, w_mod, b_mod).reshape(depth, batch + 1 + pad, N_SUB, 3, d)

    wi = w_ffn_in.astype(BF16)
    wo = w_ffn_out.astype(BF16)
    w_in_b = w_in.astype(BF16)
    wb = w_branch.astype(BF16)
    wout = w_out.astype(BF16)
    wvt = _wvt_call(w_in)
    bg = b_gate[:, None, :]
    cos, sin = _rope_tables(seq)
    ones_bd = _ones_block_diag()
    qk_gain = jnp.tile(qk_norm_g, (1, 1, LANES // HEAD_DIM))

    h = x
    hc = ctx.reshape(1, batch * ctx_len, d)
    for l in range(depth):
        last = l == depth - 1
        lam_init = 0.8 - 0.6 * math.exp(-0.3 * l)
        mod_l = lambda s: mod_all[l, :batch, s]
        mod_c = lambda s: mod_all[l, batch:batch + 1, s]
        ng = norm_g[l]

        h = _ffn_call(h, mod_l(0), ng[0:2], wi, wo, (l, 0), TOKEN_TILE)
        hc = _ffn_call(hc, mod_c(0), ng[0:2], wi, wo, (l, 0), TOKEN_TILE)

        p_lat = _proj_call(h, mod_l(1), ng[2:4], w_in_b, wvt, bg, l, qk_gain[l], ones_bd, cos, sin,
                           PROJ_TILE, True)
        p_ctx = _proj_call(hc, mod_c(1), ng[2:4], w_in_b, wvt, bg, l, qk_gain[l], ones_bd, cos, sin,
                           PROJ_TILE, False, kv_only=last)
        per_batch = lambda a: a.reshape(batch, ctx_len, a.shape[-1])
        p_ctx_b = {k: per_batch(p_ctx[k]) for k in ('qb', 'kb', 'vb', 'gb') if k in p_ctx}

        ob, ob_c = _ret_call(p_lat, p_ctx_b, ret_decay_logit[l], ret_norm_g[l][None, :], not last)
        dgain = diff_norm_g[l][None, :]
        oa, oc = _attn_call(p_lat, p_ctx, batch, ctx_len, diff_lambda[l], dgain, lam_init, ATTN_Q_TILE)
        h = _merge_call(h, mod_l(1), ng[2:4], oa, ob, oc, p_lat['gt'], wb, wout, l, TOKEN_TILE)
        if not last:
            oa_c, oc_c = _attn_call(None, p_ctx, batch, ctx_len, diff_lambda[l], dgain, lam_init, ctx_len)
            hc = _merge_call(hc, mod_c(1), ng[2:4], oa_c, ob_c.reshape(1, batch * ctx_len, BRANCH_W), oc_c,
                             p_ctx['gt'], wb, wout, l, TOKEN_TILE)

        h = _ffn_call(h, mod_l(2), ng[4:6], wi, wo, (l, 1), TOKEN_TILE)
        if not last:
            hc = _ffn_call(hc, mod_c(2), ng[4:6], wi, wo, (l, 1), TOKEN_TILE)
    return h
```

```python
import functools
import math

import numpy as np
import jax
import jax.numpy as jnp
from jax import lax
from jax.experimental import pallas as pl
from jax.experimental.pallas import tpu as pltpu

F32 = jnp.float32
BF16 = jnp.bfloat16

D_MODEL = 1024
N_SUB = 3
D_FF = 2816
EPS = 1e-6
ROPE_THETA = 10000.0
ROPE_DIM = 64
GRID_W = 64
HEAD_DIM = 64
LANES = 128
DA_HEADS = 4
RET_HEADS = 4
RET_DV = 128
GQA_HEADS = 8
GQA_KV = 2
N_BRANCH = 3
BRANCH_W = 512
QK_SCALE = HEAD_DIM ** -0.5
SOFTMAX_Q_SCALE = QK_SCALE * math.log2(math.e)
MXU_TILE = 256
RET_CHUNK = MXU_TILE
ATTN_KEY_BLOCK = MXU_TILE
TOKEN_SUB = 2 * MXU_TILE
TOKEN_TILE = 2 * TOKEN_SUB
PROJ_TILE = 2 * MXU_TILE
ATTN_Q_SUB = MXU_TILE
ATTN_Q_TILE = 4 * ATTN_Q_SUB
PROJ_ROW_SUB = MXU_TILE
ONES_ROWS = 16
DA_VROWS = LANES + ONES_ROWS
GQA_VROWS = HEAD_DIM + ONES_ROWS
VMEM_LIMIT = 56 * 1024 * 1024

_SIZES = (512, 512, 512, 256, 256, 512, 512, 512, 128, 128, N_BRANCH * D_MODEL)
_NAMES = ('a_q', 'a_k', 'a_v', 'b_q', 'b_k', 'b_v', 'b_g', 'c_q', 'c_k', 'c_v', 'gate')
_SPAN = {}
_lo = 0
for _n, _s in zip(_NAMES, _SIZES):
    _SPAN[_n] = (_lo, _lo + _s)
    _lo += _s
IN_COLS = _lo


def _cparams(n_axes):
    return pltpu.CompilerParams(dimension_semantics=("arbitrary",) * n_axes,
                                vmem_limit_bytes=VMEM_LIMIT)


def _const_spec(shape):
    nd = len(shape)
    return pl.BlockSpec(shape, lambda *_: (0,) * nd, pipeline_mode=pl.Buffered(1))


def _rms(x, g):
    return x * lax.rsqrt(jnp.mean(x * x, axis=-1, keepdims=True) + EPS) * g


def _sigmoid(x):
    return 1.0 / (1.0 + jnp.exp(-x))


def _dot(a, b):
    return jnp.dot(a, b, preferred_element_type=F32)


def _dot_nt(a, b):
    return lax.dot_general(a, b, (((1,), (1,)), ((), ())), preferred_element_type=F32)


def _dot_tn(a, b):
    return lax.dot_general(a, b, (((0,), (0,)), ((), ())), preferred_element_type=F32)


def _mod_kernel(c_ref, w_ref, b_ref, o_ref):
    c = c_ref[...]
    x = (c * _sigmoid(c)).astype(BF16)
    o_ref[0] = _dot(x, w_ref[0].astype(BF16)) + b_ref[0]


def _mod_call(c_all, w_mod, b_mod):
    depth, d, cols = w_mod.shape
    rows = c_all.shape[0]
    tn = 1536
    return pl.pallas_call(
        _mod_kernel,
        grid=(depth, cols // tn),
        in_specs=[pl.BlockSpec((rows, d), lambda l, j: (0, 0)),
                  pl.BlockSpec((1, d, tn), lambda l, j: (l, 0, j)),
                  pl.BlockSpec((1, 1, tn), lambda l, j: (l, 0, j))],
        out_specs=pl.BlockSpec((1, rows, tn), lambda l, j: (l, 0, j)),
        out_shape=jax.ShapeDtypeStruct((depth, rows, cols), F32),
        compiler_params=_cparams(2),
        name="mod",
    )(c_all, w_mod, b_mod.reshape(depth, 1, cols))


def _wvt_kernel(a_ref, c_ref, o_ref):
    n_a = a_ref.shape[1]
    o_ref[0:n_a, :] = a_ref[...].T.astype(BF16)
    o_ref[n_a:, :] = c_ref[...].T.astype(BF16)


def _wvt_call(w_in):
    depth, d, _ = w_in.shape
    (a_lo, a_hi), (c_lo, c_hi) = _SPAN['a_v'], _SPAN['c_v']
    n_a, n_c = a_hi - a_lo, c_hi - c_lo
    assert a_lo % n_a == 0 and c_lo % n_c == 0
    return pl.pallas_call(
        _wvt_kernel,
        grid=(depth,),
        in_specs=[pl.BlockSpec((None, d, n_a), lambda l: (l, 0, a_lo // n_a)),
                  pl.BlockSpec((None, d, n_c), lambda l: (l, 0, c_lo // n_c))],
        out_specs=pl.BlockSpec((None, n_a + n_c, d), lambda l: (l, 0, 0)),
        out_shape=jax.ShapeDtypeStruct((depth, n_a + n_c, d), BF16),
        compiler_params=_cparams(1),
        name="wvt",
    )(w_in, w_in)


def _row_blocks(tm, sub):
    return [slice(r, r + sub) for r in range(0, tm, sub)]


def _ffn_kernel(x_ref, mod_ref, g_ref, wi_ref, wo_ref, o_ref):
    shift, scale, gate = mod_ref[0, 0:1, :], mod_ref[0, 1:2, :], mod_ref[0, 2:3, :]
    blocks = _row_blocks(x_ref.shape[1], TOKEN_SUB)
    ys = [(_rms(x_ref[0, rows, :], g_ref[0:1, :]) * (1.0 + scale) + shift).astype(BF16) for rows in blocks]
    for rows, y in zip(blocks, ys):
        x = x_ref[0, rows, :]
        hid = []
        for c in range(D_FF // MXU_TILE):
            lo = c * MXU_TILE
            a = _dot(y, wi_ref[:, lo:lo + MXU_TILE])
            u = _dot(y, wi_ref[:, D_FF + lo:D_FF + lo + MXU_TILE])
            hid.append((a * _sigmoid(a) * u).astype(BF16))
        acc = _dot(jnp.concatenate(hid, axis=1), wo_ref[...])
        o_ref[0, rows, :] = x + 0.5 * gate * _rms(acc, g_ref[1:2, :])


def _layer_spec(stacked, lead):
    shape = stacked.shape[len(lead):]
    nd = len(shape)
    return pl.BlockSpec((None,) * len(lead) + shape, lambda *_: tuple(lead) + (0,) * nd,
                        pipeline_mode=pl.Buffered(1))


def _ffn_call(x, mod3, g2, wi, wo, lead, tm):
    groups, rows, d = x.shape
    return pl.pallas_call(
        _ffn_kernel,
        grid=(groups, rows // tm),
        in_specs=[pl.BlockSpec((1, tm, d), lambda g, i: (g, i, 0)),
                  pl.BlockSpec((1, 3, d), lambda g, i: (g, 0, 0)),
                  _const_spec(g2.shape), _layer_spec(wi, lead), _layer_spec(wo, lead)],
        out_specs=pl.BlockSpec((1, tm, d), lambda g, i: (g, i, 0)),
        out_shape=jax.ShapeDtypeStruct(x.shape, F32),
        compiler_params=_cparams(2),
        name="ffn",
    )(x, mod3, g2, wi, wo)


def _swap_half_heads(x):
    lane = lax.broadcasted_iota(jnp.int32, x.shape, 1)
    return jnp.where((lane & (HEAD_DIM - 1)) < HEAD_DIM // 2,
                     pltpu.roll(x, LANES - HEAD_DIM // 2, axis=1),
                     pltpu.roll(x, HEAD_DIM // 2, axis=1))


def _rope_cols(p, cos, sin):
    cols = []
    for j in range(p.shape[1] // LANES):
        x = p[:, j * LANES:(j + 1) * LANES]
        cols.append(x * cos + _swap_half_heads(x) * sin)
    return jnp.concatenate(cols, axis=1) if len(cols) > 1 else cols[0]


def _head_rms_cols(p, gain, ones_bd):
    cols = []
    for j in range(p.shape[1] // LANES):
        x = p[:, j * LANES:(j + 1) * LANES]
        ms = _dot((x * x).astype(BF16), ones_bd) * (1.0 / HEAD_DIM)
        cols.append(x * lax.rsqrt(ms + EPS) * gain)
    return jnp.concatenate(cols, axis=1) if len(cols) > 1 else cols[0]


def _dup_heads(x):
    lane = lax.broadcasted_iota(jnp.int32, x.shape, 1)
    r = pltpu.roll(x, HEAD_DIM, axis=1)
    low = lane < HEAD_DIM
    return jnp.concatenate([jnp.where(low, x, r), jnp.where(low, r, x)], axis=1)


def _proj_kernel(x_ref, mod_ref, g_ref, w_ref, wvt_ref, bg_ref, qkg_ref, bd_ref, cos_ref, sin_ref,
                 *out_refs, rope, names):
    out = dict(zip(names, out_refs))
    shift, scale = mod_ref[0, 0:1, :], mod_ref[0, 1:2, :]
    bd = bd_ref[...]
    ones = jnp.ones((ONES_ROWS, PROJ_ROW_SUB), BF16)
    blocks = _row_blocks(x_ref.shape[1], PROJ_ROW_SUB)
    ys = [(_rms(x_ref[0, rows, :], g_ref[0:1, :]) * (1.0 + scale) + shift).astype(BF16) for rows in blocks]
    for rows, y in zip(blocks, ys):
        cos, sin = cos_ref[rows, :], sin_ref[rows, :]

        def proj(name):
            lo, hi = _SPAN[name]
            return _dot(y, w_ref[:, lo:hi])

        def pos(p):
            return _rope_cols(p, cos, sin) if rope else p

        if 'qc' in out:
            cq = proj('c_q')
        ck = proj('c_k')
        if 'qa' in out:
            out['qa'][0, rows, :] = (pos(proj('a_q')) * SOFTMAX_Q_SCALE).astype(BF16)
        out['ka'][0, rows, :] = pos(proj('a_k')).astype(BF16)
        if 'qc' in out:
            out['qc'][0, rows, :] = (pos(_head_rms_cols(cq, qkg_ref[0:1, :], bd))
                                     * SOFTMAX_Q_SCALE).astype(BF16)
        out['kc'][0, rows, :] = _dup_heads(pos(_head_rms_cols(ck, qkg_ref[1:2, :], bd))).astype(BF16)
        if 'qb' in out:
            out['qb'][0, rows, :] = pos(proj('b_q')) * QK_SCALE
        out['kb'][0, rows, :] = pos(proj('b_k'))
        out['vb'][0, rows, :] = proj('b_v').astype(BF16)
        if 'gb' in out:
            g = proj('b_g')
            out['gb'][0, rows, :] = (g * _sigmoid(g)).astype(BF16)
        if 'gt' in out:
            glo = _SPAN['gate'][0]
            for i in range(N_BRANCH):
                lo = glo + i * D_MODEL
                pre = _dot(y, w_ref[:, lo:lo + D_MODEL]) + bg_ref[:, i * D_MODEL:(i + 1) * D_MODEL]
                out['gt'][0, rows, i * D_MODEL:(i + 1) * D_MODEL] = _sigmoid(pre).astype(BF16)
        vt = _dot_nt(wvt_ref[...], y).astype(BF16)
        vat_ref, vct_ref = out['vat'], out['vct']
        for c in range(DA_HEADS):
            vat_ref[0, c * DA_VROWS:c * DA_VROWS + LANES, rows] = vt[c * LANES:(c + 1) * LANES]
            vat_ref[0, c * DA_VROWS + LANES:(c + 1) * DA_VROWS, rows] = ones
        for g_ in range(GQA_KV):
            src = DA_HEADS * LANES + g_ * HEAD_DIM
            vct_ref[0, g_ * GQA_VROWS:g_ * GQA_VROWS + HEAD_DIM, rows] = vt[src:src + HEAD_DIM]
            vct_ref[0, g_ * GQA_VROWS + HEAD_DIM:(g_ + 1) * GQA_VROWS, rows] = ones


_PROJ_OUT = (('qa', 512, BF16), ('ka', 512, BF16), ('qb', 256, F32), ('kb', 256, F32),
             ('vb', 512, BF16), ('gb', 512, BF16), ('qc', 512, BF16), ('kc', 256, BF16),
             ('gt', N_BRANCH * D_MODEL, BF16))
_PROJ_OUT_T = (('vat', DA_HEADS * DA_VROWS), ('vct', GQA_KV * GQA_VROWS))
_PROJ_KV_ONLY = ('ka', 'kb', 'vb', 'kc', 'vat', 'vct')


def _proj_call(x, mod3, g2, w_in, wvt, b_gate, layer, qk_gain, ones_bd, cos, sin, tm, rope, kv_only=False):
    groups, rows, d = x.shape
    tok = lambda w: pl.BlockSpec((1, tm, w), lambda g, i: (g, i, 0))
    tok_t = lambda r: pl.BlockSpec((1, r, tm), lambda g, i: (g, 0, i))
    pos_map = (lambda g, i: (i, 0)) if rope else (lambda g, i: (0, 0))
    keep = lambda n: not kv_only or n in _PROJ_KV_ONLY
    out_tok = [(n, w, dt) for n, w, dt in _PROJ_OUT if keep(n)]
    out_t = [(n, r) for n, r in _PROJ_OUT_T if keep(n)]
    names = tuple([n for n, _, _ in out_tok] + [n for n, _ in out_t])
    outs = pl.pallas_call(
        functools.partial(_proj_kernel, rope=rope, names=names),
        grid=(groups, rows // tm),
        in_specs=[tok(d),
                  pl.BlockSpec((1, 3, d), lambda g, i: (g, 0, 0)),
                  _const_spec(g2.shape), _layer_spec(w_in, (layer,)), _layer_spec(wvt, (layer,)),
                  _layer_spec(b_gate, (layer,)),
                  _const_spec(qk_gain.shape), _const_spec(ones_bd.shape),
                  pl.BlockSpec((tm, LANES), pos_map), pl.BlockSpec((tm, LANES), pos_map)],
        out_specs=[tok(w) for _, w, _ in out_tok] + [tok_t(r) for _, r in out_t],
        out_shape=([jax.ShapeDtypeStruct((groups, rows, w), dt) for _, w, dt in out_tok]
                   + [jax.ShapeDtypeStruct((groups, r, rows), BF16) for _, r in out_t]),
        compiler_params=_cparams(2),
        name="proj_rope" if rope else ("proj_ctx_kv" if kv_only else "proj_ctx"),
    )(x, mod3, g2, w_in, wvt, b_gate, qk_gain, ones_bd, cos, sin)
    return dict(zip(names, outs))


def _log_sigmoid(x):
    return jnp.minimum(x, 0.0) - jnp.log(1.0 + jnp.exp(-jnp.abs(x)))


def _split_heads(col):
    lane = lax.broadcasted_iota(jnp.int32, col.shape, 1)
    zero = jnp.zeros_like(col)
    return jnp.concatenate([jnp.where(lane < HEAD_DIM, col, zero),
                            jnp.where(lane >= HEAD_DIM, col, zero)], axis=0)


def _ret_kernel(*refs, n_chunks, with_ctx_out):
    if with_ctx_out:
        (q_ref, k_ref, v_ref, g_ref, kc_ref, vc_ref, qc_ref, gc_ref, logit_ref, gain_ref,
         o_ref, oc_ref, mask_ref, dec_ref) = refs
    else:
        q_ref, k_ref, v_ref, g_ref, kc_ref, vc_ref, logit_ref, gain_ref, o_ref, mask_ref, dec_ref = refs
    C = RET_CHUNK
    gain = gain_ref[...]
    log_decay = lambda d, h: _log_sigmoid(logit_ref[d:d + 1, h:h + 1])

    @pl.when(pl.program_id(0) == 0)
    def _():
        di = lax.broadcasted_iota(jnp.int32, (C, C), 0) - lax.broadcasted_iota(jnp.int32, (C, C), 1)
        dpos = jnp.maximum(di, 0).astype(F32)
        dneg = jnp.maximum(-di, 0).astype(F32)
        pos = lax.broadcasted_iota(jnp.int32, (C, 1), 0).astype(F32)
        for h in range(RET_HEADS):
            lf, lb = log_decay(0, h), log_decay(1, h)
            mask_ref[h] = (jnp.where(di >= 0, jnp.exp(lf * dpos), 0.0)
                           + jnp.where(di <= 0, jnp.exp(lb * dneg), 0.0))
            dec_ref[4 * h + 0] = jnp.exp(lf * (pos + 1.0))
            dec_ref[4 * h + 1] = jnp.exp(lb * (C - pos))
            dec_ref[4 * h + 2] = jnp.exp(lf * (C - 1.0 - pos))
            dec_ref[4 * h + 3] = jnp.exp(lb * pos)

    tabs = [dict(mask=mask_ref[h], q_f=dec_ref[4 * h + 0], q_b=dec_ref[4 * h + 1],
                 k_f=dec_ref[4 * h + 2], k_b=dec_ref[4 * h + 3],
                 s_f=jnp.exp(log_decay(0, h) * C), s_b=jnp.exp(log_decay(1, h) * C))
            for h in range(RET_HEADS)]

    def chunk_states(kcol, vfull, h):
        v = vfull[:, h * RET_DV:(h + 1) * RET_DV]
        t = tabs[h]
        return (_dot_tn((kcol * t['k_f']).astype(BF16), v), _dot_tn((kcol * t['k_b']).astype(BF16), v))

    def finish(o, gate):
        return (_rms(o, gain) * gate.astype(F32)).astype(BF16)

    kc_all, vc_all = kc_ref[0], vc_ref[0]
    init = [chunk_states(kc_all[:, (h // 2) * LANES:(h // 2 + 1) * LANES], vc_all, h) for h in range(RET_HEADS)]

    sums = []
    for c in range(n_chunks):
        kc_ = k_ref[0, c * C:(c + 1) * C, :]
        vc_ = v_ref[0, c * C:(c + 1) * C, :]
        sums.append([chunk_states(kc_[:, (h // 2) * LANES:(h // 2 + 1) * LANES], vc_, h) for h in range(RET_HEADS)])
    fwd = [[None] * RET_HEADS for _ in range(n_chunks)]
    bwd = [[None] * RET_HEADS for _ in range(n_chunks)]
    for h in range(RET_HEADS):
        s = init[h][0]
        for c in range(n_chunks):
            fwd[c][h] = s
            s = s * tabs[h]['s_f'] + sums[c][h][0]
        s = init[h][1]
        for c in reversed(range(n_chunks)):
            bwd[c][h] = s
            s = s * tabs[h]['s_b'] + sums[c][h][1]

    def chunk_out(q, k, v, states):
        outs = []
        for j in range(RET_HEADS // 2):
            qs = _split_heads(q[:, j * LANES:(j + 1) * LANES])
            s = _dot_nt(qs.astype(BF16), k[:, j * LANES:(j + 1) * LANES].astype(BF16))
            for hh in range(2):
                h = 2 * j + hh
                t = tabs[h]
                att = (s[hh * C:(hh + 1) * C] * t['mask']).astype(BF16)
                o = _dot(att, v[:, h * RET_DV:(h + 1) * RET_DV])
                if states is not None:
                    qm = qs[hh * C:(hh + 1) * C]
                    sf, sb = states[h]
                    o = o + _dot((qm * t['q_f']).astype(BF16), sf.astype(BF16))
                    o = o + _dot((qm * t['q_b']).astype(BF16), sb.astype(BF16))
                outs.append(o)
        return outs

    for c in range(n_chunks):
        rows = slice(c * C, (c + 1) * C)
        outs = chunk_out(q_ref[0, rows, :], k_ref[0, rows, :], v_ref[0, rows, :],
                         [(fwd[c][h], bwd[c][h]) for h in range(RET_HEADS)])
        for h in range(RET_HEADS):
            cols = slice(h * RET_DV, (h + 1) * RET_DV)
            o_ref[0, rows, cols] = finish(outs[h], g_ref[0, rows, cols])

    if with_ctx_out:
        outs = chunk_out(qc_ref[0], kc_all, vc_all, None)
        for h in range(RET_HEADS):
            cols = slice(h * RET_DV, (h + 1) * RET_DV)
            oc_ref[0, :, cols] = finish(outs[h], gc_ref[0, :, cols])


def _ret_call(pl_, pc_, logit, gain, with_ctx_out):
    b, n, _ = pl_['qb'].shape
    lc = pc_['kb'].shape[1]
    assert lc == RET_CHUNK and n % RET_CHUNK == 0
    full = lambda a: pl.BlockSpec((1,) + a.shape[1:], lambda i: (i, 0, 0))
    ins = [pl_['qb'], pl_['kb'], pl_['vb'], pl_['gb'], pc_['kb'], pc_['vb']]
    if with_ctx_out:
        ins += [pc_['qb'], pc_['gb']]
    in_specs = [full(a) for a in ins] + [_const_spec(logit.shape), _const_spec(gain.shape)]
    out_shape = [jax.ShapeDtypeStruct((b, n, RET_HEADS * RET_DV), BF16)]
    if with_ctx_out:
        out_shape.append(jax.ShapeDtypeStruct((b, lc, RET_HEADS * RET_DV), BF16))
    outs = pl.pallas_call(
        functools.partial(_ret_kernel, n_chunks=n // RET_CHUNK, with_ctx_out=with_ctx_out),
        grid=(b,),
        in_specs=in_specs,
        out_specs=[pl.BlockSpec((1,) + s.shape[1:], lambda i: (i, 0, 0)) for s in out_shape],
        out_shape=out_shape,
        scratch_shapes=[pltpu.VMEM((RET_HEADS, RET_CHUNK, RET_CHUNK), F32),
                        pltpu.VMEM((RET_HEADS * 4, RET_CHUNK, 1), F32)],
        compiler_params=_cparams(1),
        name="retention_ctx" if with_ctx_out else "retention",
    )(*ins, logit, gain)
    return outs if with_ctx_out else (outs[0], None)


def _fold_rows(x, op):
    r, n = x.shape
    return op(x.reshape(r // 8, 8, n), axis=0)


def _attn_kernel(*refs, has_lat, lam_init):
    if has_lat:
        (qa_ref, qc_ref, ka_ref, kc_ref, vat_ref, vct_ref, kac_ref, kcc_ref, vact_ref, vcct_ref,
         dl_ref, dg_ref, oa_ref, oc_ref, s_ref) = refs
    else:
        qa_ref, qc_ref, kac_ref, kcc_ref, vact_ref, vcct_ref, dl_ref, dg_ref, oa_ref, oc_ref, s_ref = refs

    dl = dl_ref[...]
    lam = (jnp.exp(jnp.sum(dl[0:1] * dl[1:2], axis=-1, keepdims=True))
           - jnp.exp(jnp.sum(dl[2:3] * dl[3:4], axis=-1, keepdims=True)) + lam_init)
    dgain = dg_ref[...] * (1.0 - lam_init)
    tq = min(ATTN_Q_SUB, qa_ref.shape[1])
    n_cols = DA_HEADS + GQA_HEADS // 2
    units = [(t, c) for t in range(qa_ref.shape[1] // tq) for c in range(n_cols)]
    t_lat = ka_ref.shape[1] if has_lat else 0
    t_ctx = kac_ref.shape[1]
    kb = ATTN_KEY_BLOCK
    assert t_lat % kb == 0 and t_ctx % kb == 0
    blocks = [(False, r) for r in range(0, t_lat, kb)] + [(True, r) for r in range(0, t_ctx, kb)]

    def column(u):
        t, c = units[u]
        qrows = slice(t * tq, (t + 1) * tq)
        if c < DA_HEADS:
            cols = slice(c * LANES, (c + 1) * LANES)
            return (qa_ref[0, qrows, cols], (ka_ref if has_lat else None, kac_ref), cols,
                    (vat_ref if has_lat else None, vact_ref), slice(c * DA_VROWS, (c + 1) * DA_VROWS), LANES)
        j = c - DA_HEADS
        g = j // 2
        return (qc_ref[0, qrows, slice(j * LANES, (j + 1) * LANES)], (kc_ref if has_lat else None, kcc_ref),
                slice(g * LANES, (g + 1) * LANES),
                (vct_ref if has_lat else None, vcct_ref), slice(g * GQA_VROWS, (g + 1) * GQA_VROWS), HEAD_DIM)

    runtime_zero = jnp.minimum(pl.program_id(1), 0)

    def score_block(u, qs, i):
        _, krefs, kcols, _, _, _ = column(u)
        is_ctx, r = blocks[i]
        s = _dot(krefs[is_ctx][0, r:r + kb, kcols], qs)
        s_ref[u % 2 + runtime_zero, i * kb:(i + 1) * kb, :] = s
        return _fold_rows(s, jnp.max)

    def value_block(u, m, i):
        _, _, _, vrefs, vrows, _ = column(u)
        is_ctx, r = blocks[i]
        e = jnp.exp2(s_ref[u % 2 + runtime_zero, i * kb:(i + 1) * kb, :] - m)
        return _dot(vrefs[is_ctx][0, vrows, r:r + kb], e.astype(BF16))

    def finish(u, o_ext):
        t, c = units[u]
        qrows = slice(t * tq, (t + 1) * tq)
        dv = column(u)[5]
        o = o_ext[:dv] * (1.0 / o_ext[dv:dv + 1])
        o1, o2 = o[:, :tq], o[:, tq:]
        if c < DA_HEADS:
            d = o1 - lam * o2
            d = d * lax.rsqrt(jnp.mean(d * d, axis=0, keepdims=True) + EPS)
            oa_ref[0, qrows, c * LANES:(c + 1) * LANES] = (d.T * dgain).astype(BF16)
        else:
            j = c - DA_HEADS
            oc_ref[0, qrows, j * LANES:(j + 1) * LANES] = jnp.concatenate([o1, o2], axis=0).T.astype(BF16)

    def tree(op, xs):
        return functools.reduce(op, xs)

    def split_heads_t(qcol):
        qt = qcol.astype(F32).T
        row = lax.broadcasted_iota(jnp.int32, qt.shape, 0)
        return jnp.concatenate([jnp.where(row < HEAD_DIM, qt, 0.0), jnp.where(row >= HEAD_DIM, qt, 0.0)],
                               axis=1).astype(BF16)

    qs = split_heads_t(column(0)[0])
    m = jnp.max(tree(jnp.maximum, [score_block(0, qs, i) for i in range(len(blocks))]), axis=0, keepdims=True)
    for u in range(len(units)):
        nxt = u + 1 < len(units)
        if nxt:
            qs_next = split_heads_t(column(u + 1)[0])
        maxes, outs = [], []
        for i in range(len(blocks)):
            if nxt:
                maxes.append(score_block(u + 1, qs_next, i))
            outs.append(value_block(u, m, i))
        finish(u, tree(lambda a, b: a + b, outs))
        if nxt:
            m = jnp.max(tree(jnp.maximum, maxes), axis=0, keepdims=True)


def _attn_call(pl_, pc_, batch, ctx_len, diff_lambda, diff_gain, lam_init, tq):
    has_lat = pl_ is not None
    ctx_tok = lambda w: pl.BlockSpec((1, ctx_len, w), lambda i, j: (0, i, 0))
    ctx_t = lambda r: pl.BlockSpec((1, r, ctx_len), lambda i, j: (0, 0, i))
    if has_lat:
        n = pl_['qa'].shape[1]
        qspec = lambda w: pl.BlockSpec((1, tq, w), lambda i, j: (i, j, 0))
        whole = lambda a: pl.BlockSpec((1,) + a.shape[1:], lambda i, j: (i, 0, 0))
        kvl = [pl_['ka'], pl_['kc'], pl_['vat'], pl_['vct']]
        ins = [pl_['qa'], pl_['qc']] + kvl
        in_specs = [qspec(512), qspec(512)] + [whole(a) for a in kvl]
        grid = (batch, n // tq)
        out_rows, t_total = (batch, n), n + ctx_len
    else:
        assert tq == ctx_len
        qspec = ctx_tok
        ins = [pc_['qa'], pc_['qc']]
        in_specs = [qspec(512), qspec(512)]
        grid = (batch, 1)
        out_rows, t_total = (1, batch * ctx_len), ctx_len
    ins += [pc_['ka'], pc_['kc'], pc_['vat'], pc_['vct'], diff_lambda, diff_gain]
    in_specs += [ctx_tok(512), ctx_tok(256), ctx_t(DA_HEADS * DA_VROWS), ctx_t(GQA_KV * GQA_VROWS),
                 _const_spec(diff_lambda.shape), _const_spec(diff_gain.shape)]
    return pl.pallas_call(
        functools.partial(_attn_kernel, has_lat=has_lat, lam_init=lam_init),
        grid=grid,
        in_specs=in_specs,
        out_specs=[qspec(BRANCH_W), qspec(BRANCH_W)],
        out_shape=[jax.ShapeDtypeStruct(out_rows + (BRANCH_W,), BF16)] * 2,
        scratch_shapes=[pltpu.VMEM((2, t_total, 2 * min(ATTN_Q_SUB, tq)), F32)],
        compiler_params=_cparams(2),
        name="attn_lat" if has_lat else "attn_ctx",
    )(*ins)


def _merge_kernel(h_ref, mod_ref, g_ref, oa_ref, ob_ref, oc_ref, gt_ref, wb_ref, wo_ref, o_ref):
    for rows in _row_blocks(h_ref.shape[1], TOKEN_SUB):
        merged = None
        for i, br in enumerate((oa_ref, ob_ref, oc_ref)):
            p = _dot(br[0, rows, :], wb_ref[i]) * gt_ref[0, rows, i * D_MODEL:(i + 1) * D_MODEL].astype(F32)
            merged = p if merged is None else merged + p
        out = _dot(merged.astype(BF16), wo_ref[...])
        o_ref[0, rows, :] = h_ref[0, rows, :] + mod_ref[0, 2:3, :] * _rms(out, g_ref[1:2, :])


def _merge_call(h, mod3, g2, oa, ob, oc, gt, w_branch, w_out, layer, tm):
    groups, rows, d = h.shape
    tok = lambda w: pl.BlockSpec((1, tm, w), lambda g, i: (g, i, 0))
    return pl.pallas_call(
        _merge_kernel,
        grid=(groups, rows // tm),
        in_specs=[tok(d), pl.BlockSpec((1, 3, d), lambda g, i: (g, 0, 0)), _const_spec(g2.shape),
                  tok(BRANCH_W), tok(BRANCH_W), tok(BRANCH_W), tok(N_BRANCH * d),
                  _layer_spec(w_branch, (layer,)), _layer_spec(w_out, (layer,))],
        out_specs=tok(d),
        out_shape=jax.ShapeDtypeStruct(h.shape, F32),
        compiler_params=_cparams(2),
        name="merge",
    )(h, mod3, g2, oa, ob, oc, gt, w_branch, w_out)


def _rope_tables(n_tok):
    t = np.arange(n_tok)
    n_freq = ROPE_DIM // 4
    inv = ROPE_THETA ** (-np.arange(n_freq, dtype=np.float64) / n_freq)
    ang = np.concatenate([(t // GRID_W)[:, None] * inv, (t % GRID_W)[:, None] * inv], axis=-1)
    cos, sin = np.cos(ang), np.sin(ang)
    cos128 = np.tile(cos, (1, LANES // (ROPE_DIM // 2)))
    sin128 = np.tile(np.concatenate([-sin, sin], axis=-1), (1, LANES // ROPE_DIM))
    return jnp.asarray(cos128, F32), jnp.asarray(sin128, F32)


def _ones_block_diag():
    i = np.arange(LANES)
    return jnp.asarray((i[:, None] // HEAD_DIM) == (i[None, :] // HEAD_DIM), BF16)


def kernel(x, c, ctx, c_ctx, w_mod, b_mod, norm_g, w_ffn_in, w_ffn_out, w_in, b_gate, diff_lambda,
           diff_norm_g, ret_decay_logit, ret_norm_g, qk_norm_g, w_branch, w_out):
    batch, seq, d = x.shape
    ctx_len = ctx.shape[1]
    depth = w_mod.shape[0]
    assert d == D_MODEL and seq % TOKEN_TILE == 0 and (batch * ctx_len) % TOKEN_TILE == 0

    pad = (-(batch + 1)) % 8
    c_all = jnp.concatenate([c, c_ctx[None, :], jnp.zeros((pad, d), F32)], axis=0)
    mod_all = _mod_call(c_all, w_mod, b_mod).reshape(depth, batch + 1 + pad, N_SUB, 3, d)

    wi = w_ffn_in.astype(BF16)
    wo = w_ffn_out.astype(BF16)
    w_in_b = w_in.astype(BF16)
    wb = w_branch.astype(BF16)
    wout = w_out.astype(BF16)
    wvt = _wvt_call(w_in)
    bg = b_gate[:, None, :]
    cos, sin = _rope_tables(seq)
    ones_bd = _ones_block_diag()
    qk_gain = jnp.tile(qk_norm_g, (1, 1, LANES // HEAD_DIM))

    h = x
    hc = ctx.reshape(1, batch * ctx_len, d)
    for l in range(depth):
        last = l == depth - 1
        lam_init = 0.8 - 0.6 * math.exp(-0.3 * l)
        mod_l = lambda s: mod_all[l, :batch, s]
        mod_c = lambda s: mod_all[l, batch:batch + 1, s]
        ng = norm_g[l]

        h = _ffn_call(h, mod_l(0), ng[0:2], wi, wo, (l, 0), TOKEN_TILE)
        hc = _ffn_call(hc, mod_c(0), ng[0:2], wi, wo, (l, 0), TOKEN_TILE)

        p_lat = _proj_call(h, mod_l(1), ng[2:4], w_in_b, wvt, bg, l, qk_gain[l], ones_bd, cos, sin,
                           PROJ_TILE, True)
        p_ctx = _proj_call(hc, mod_c(1), ng[2:4], w_in_b, wvt, bg, l, qk_gain[l], ones_bd, cos, sin,
                           PROJ_TILE, False, kv_only=last)
        per_batch = lambda a: a.reshape(batch, ctx_len, a.shape[-1])
        p_ctx_b = {k: per_batch(p_ctx[k]) for k in ('qb', 'kb', 'vb', 'gb') if k in p_ctx}

        ob, ob_c = _ret_call(p_lat, p_ctx_b, ret_decay_logit[l], ret_norm_g[l][None, :], not last)
        dgain = diff_norm_g[l][None, :]
        oa, oc = _attn_call(p_lat, p_ctx, batch, ctx_len, diff_lambda[l], dgain, lam_init, ATTN_Q_TILE)
        h = _merge_call(h, mod_l(1), ng[2:4], oa, ob, oc, p_lat['gt'], wb, wout, l, TOKEN_TILE)
        if not last:
            oa_c, oc_c = _attn_call(None, p_ctx, batch, ctx_len, diff_lambda[l], dgain, lam_init, ctx_len)
            hc = _merge_call(hc, mod_c(1), ng[2:4], oa_c, ob_c.reshape(1, batch * ctx_len, BRANCH_W), oc_c,
                             p_ctx['gt'], wb, wout, l, TOKEN_TILE)

        h = _ffn_call(h, mod_l(2), ng[4:6], wi, wo, (l, 1), TOKEN_TILE)
        if not last:
            hc = _ffn_call(hc, mod_c(2), ng[4:6], wi, wo, (l, 1), TOKEN_TILE)
    return h
```
